```python
import math
import jax, jax.numpy as jnp
from jax import lax
import numpy as np

D_MODEL = 1024
BATCH = 8
SEQ = 2048
DEPTH = 4

GRID_W = 64
CTX_LEN = 256
N_MIXERS = 2
N_S5_LAYERS = (DEPTH + 1) // 2
N_ATTN_LAYERS = DEPTH // 2
S5_GROUP = 16
S5_GROUPS = D_MODEL // S5_GROUP
S5_STATE = 64
DT_MIN = 0.001
DT_MAX = 0.1
N_HEADS = 16
N_KV_HEADS = 4
HEAD_DIM = D_MODEL // N_HEADS
Q_PER_KV = N_HEADS // N_KV_HEADS
ROPE_AXIS_DIM = HEAD_DIM // 2
ROPE_THETA = 10000.0
Q_BLOCK = 128
QKV_DIM = (N_HEADS + 2 * N_KV_HEADS) * HEAD_DIM
D_FF = 4 * D_MODEL
NORM_EPS = 1e-6

kernel_name = "hybrid_s5_gqa_dit_prefix_trunk"


def _rmsnorm(x, g):
    xf = x.astype(jnp.float32)
    r = lax.rsqrt(jnp.mean(xf * xf, axis=-1, keepdims=True) + NORM_EPS)
    return (xf * r).astype(x.dtype) * g


def _modulate(h, shift, scale):
    return h * (1.0 + scale) + shift


def _sq_relu_mlp(h, w1, w2):
    return jnp.square(jax.nn.relu(h @ w1)) @ w2


def _diag_scan(a_re, a_im, b_re, b_im, reverse):
    L = b_re.shape[1]
    a_re = jnp.broadcast_to(a_re, (1, L) + a_re.shape)
    a_im = jnp.broadcast_to(a_im, (1, L) + a_im.shape)

    def combine(e1, e2):
        ar1, ai1, br1, bi1 = e1
        ar2, ai2, br2, bi2 = e2
        return (ar2 * ar1 - ai2 * ai1,
                ar2 * ai1 + ai2 * ar1,
                ar2 * br1 - ai2 * bi1 + br2,
                ar2 * bi1 + ai2 * br1 + bi2)

    _, _, h_re, h_im = lax.associative_scan(combine, (a_re, a_im, b_re, b_im), axis=1, reverse=reverse)
    return h_re, h_im


def _s5_mixer(h, hc, a_re, a_im, log_dt, b_re, b_im, c_re, c_im, d, w_glu, need_ctx_out):
    Bsz, L, _ = h.shape
    CL = hc.shape[1]
    f32 = jnp.float32
    u = h.astype(f32).reshape(Bsz, L, S5_GROUPS, S5_GROUP)
    uc = hc.astype(f32).reshape(Bsz, CL, S5_GROUPS, S5_GROUP)
    d32 = d.astype(f32)
    y = d32 * h.astype(f32)
    yc = d32 * hc.astype(f32) if need_ctx_out else None
    for dirn in range(2):
        reverse = dirn == 1
        ar = a_re[dirn].astype(f32)
        ai = a_im[dirn].astype(f32)
        dt = jnp.exp(log_dt[dirn].astype(f32))[:, None]
        mag = jnp.exp(dt * ar)
        abar_re = mag * jnp.cos(dt * ai)
        abar_im = mag * jnp.sin(dt * ai)
        den = ar * ar + ai * ai
        nr = abar_re - 1.0
        f_re = (nr * ar + abar_im * ai) / den
        f_im = (abar_im * ar - nr * ai) / den
        br = b_re[dirn].astype(f32)
        bi = b_im[dirn].astype(f32)
        bb_re = f_re[..., None] * br - f_im[..., None] * bi
        bb_im = f_re[..., None] * bi + f_im[..., None] * br
        cr = c_re[dirn].astype(f32)
        ci = c_im[dirn].astype(f32)
        bu_re = jnp.einsum('blgh,gph->blgp', uc, bb_re)
        bu_im = jnp.einsum('blgh,gph->blgp', uc, bb_im)
        hc_re, hc_im = _diag_scan(abar_re, abar_im, bu_re, bu_im, reverse)
        pos_c = 0 if reverse else CL - 1
        init_re = hc_re[:, pos_c]
        init_im = hc_im[:, pos_c]
        if need_ctx_out:
            yc = yc + (jnp.einsum('blgp,ghp->blgh', hc_re, cr)
                       - jnp.einsum('blgp,ghp->blgh', hc_im, ci)).reshape(Bsz, CL, D_MODEL)
        bu_re = jnp.einsum('blgh,gph->blgp', u, bb_re)
        bu_im = jnp.einsum('blgh,gph->blgp', u, bb_im)
        pos = L - 1 if reverse else 0
        bu_re = bu_re.at[:, pos].add(abar_re * init_re - abar_im * init_im)
        bu_im = bu_im.at[:, pos].add(abar_re * init_im + abar_im * init_re)
        hl_re, hl_im = _diag_scan(abar_re, abar_im, bu_re, bu_im, reverse)
        y = y + (jnp.einsum('blgp,ghp->blgh', hl_re, cr)
                 - jnp.einsum('blgp,ghp->blgh', hl_im, ci)).reshape(Bsz, L, D_MODEL)

    def glu(t):
        z = jax.nn.gelu(t).astype(h.dtype) @ w_glu
        za, zb = jnp.split(z, 2, axis=-1)
        return za * jax.nn.sigmoid(zb)

    return glu(y), (glu(yc) if need_ctx_out else None)


def _rope_axis(x, ang):
    x1, x2 = jnp.split(x, 2, axis=-1)
    shape = (1, ang.shape[0]) + (1,) * (x.ndim - 3) + (ang.shape[1],)
    cos = jnp.cos(ang).reshape(shape).astype(x.dtype)
    sin = jnp.sin(ang).reshape(shape).astype(x.dtype)
    return jnp.concatenate([x1 * cos - x2 * sin, x2 * cos + x1 * sin], axis=-1)


def _rope_2d(x, ang_row, ang_col):
    return jnp.concatenate([_rope_axis(x[..., :ROPE_AXIS_DIM], ang_row),
                            _rope_axis(x[..., ROPE_AXIS_DIM:], ang_col)], axis=-1)


def _attend(q, k, v):
    s = jnp.einsum('bqkgd,bskd->bkgqs', q, k).astype(jnp.float32) * (HEAD_DIM ** -0.5)
    p = jax.nn.softmax(s, axis=-1).astype(v.dtype)
    return jnp.einsum('bkgqs,bskd->bqkgd', p, v)


def _attn_mixer(h, hc, w_qkv, q_g, k_g, w_o, ang_row, ang_col, need_ctx_out):
    Bsz, L, _ = h.shape
    CL = hc.shape[1]

    def project(t):
        n = t.shape[1]
        qkv = t @ w_qkv
        q = qkv[..., :N_HEADS * HEAD_DIM].reshape(Bsz, n, N_KV_HEADS, Q_PER_KV, HEAD_DIM)
        k = qkv[..., N_HEADS * HEAD_DIM:(N_HEADS + N_KV_HEADS) * HEAD_DIM].reshape(Bsz, n, N_KV_HEADS, HEAD_DIM)
        v = qkv[..., (N_HEADS + N_KV_HEADS) * HEAD_DIM:].reshape(Bsz, n, N_KV_HEADS, HEAD_DIM)
        return _rmsnorm(q, q_g), _rmsnorm(k, k_g), v

    q, k, v = project(h)
    qc, kc, vc = project(hc)
    q = _rope_2d(q, ang_row, ang_col)
    k = _rope_2d(k, ang_row, ang_col)
    k_all = jnp.concatenate([kc, k], axis=1)
    v_all = jnp.concatenate([vc, v], axis=1)
    nb = L // Q_BLOCK
    q_blocks = q.reshape(Bsz, nb, Q_BLOCK, N_KV_HEADS, Q_PER_KV, HEAD_DIM).transpose(1, 0, 2, 3, 4, 5)
    o = lax.map(lambda qb: _attend(qb, k_all, v_all), q_blocks)
    o = o.transpose(1, 0, 2, 3, 4, 5).reshape(Bsz, L, D_MODEL)
    y = o @ w_o
    yc = None
    if need_ctx_out:
        yc = _attend(qc, kc, vc).reshape(Bsz, CL, D_MODEL) @ w_o
    return y, yc


def setup_inputs(seed: int = 0) -> dict:
    key = jax.random.key(seed)
    ks = jax.random.split(key, 24)
    D = D_MODEL
    G, P, H = S5_GROUPS, S5_STATE, S5_GROUP
    nrm = jax.random.normal
    return {
        "x": nrm(ks[0], (BATCH, SEQ, D), jnp.float32),
        "c": nrm(ks[1], (BATCH, D), jnp.float32),
        "ctx": nrm(ks[2], (BATCH, CTX_LEN, D), jnp.float32),
        "c_ctx": nrm(ks[3], (D,), jnp.float32),
        "ada_w": nrm(ks[4], (DEPTH, D, 6 * D), jnp.float32) * (0.5 * D ** -0.5),
        "ada_b": nrm(ks[5], (DEPTH, 6 * D), jnp.float32) * 0.02,
        "norm_mix_g": 1.0 + 0.02 * nrm(ks[6], (DEPTH, D), jnp.float32),
        "norm_ffn_g": 1.0 + 0.02 * nrm(ks[7], (DEPTH, D), jnp.float32),
        "s5_a_re": -0.5 + 0.01 * nrm(ks[8], (N_S5_LAYERS, 2, G, P), jnp.float32),
        "s5_a_im": math.pi * jnp.arange(P, dtype=jnp.float32) + 0.01 * nrm(ks[9], (N_S5_LAYERS, 2, G, P), jnp.float32),
        "s5_log_dt": jax.random.uniform(ks[10], (N_S5_LAYERS, 2, G), jnp.float32, math.log(DT_MIN), math.log(DT_MAX)),
        "s5_b_re": nrm(ks[11], (N_S5_LAYERS, 2, G, P, H), jnp.float32) * (2 * H) ** -0.5,
        "s5_b_im": nrm(ks[12], (N_S5_LAYERS, 2, G, P, H), jnp.float32) * (2 * H) ** -0.5,
        "s5_c_re": nrm(ks[13], (N_S5_LAYERS, 2, G, H, P), jnp.float32) * P ** -0.5,
        "s5_c_im": nrm(ks[14], (N_S5_LAYERS, 2, G, H, P), jnp.float32) * P ** -0.5,
        "s5_d": nrm(ks[15], (N_S5_LAYERS, D), jnp.float32),
        "s5_w_glu": nrm(ks[16], (N_S5_LAYERS, D, 2 * D), jnp.float32) * D ** -0.5,
        "attn_w_qkv": nrm(ks[17], (N_ATTN_LAYERS, D, QKV_DIM), jnp.float32) * D ** -0.5,
        "attn_q_g": 1.0 + 0.02 * nrm(ks[18], (N_ATTN_LAYERS, HEAD_DIM), jnp.float32),
        "attn_k_g": 1.0 + 0.02 * nrm(ks[19], (N_ATTN_LAYERS, HEAD_DIM), jnp.float32),
        "attn_w_o": nrm(ks[20], (N_ATTN_LAYERS, D, D), jnp.float32) * D ** -0.5,
        "ffn_w1": nrm(ks[21], (DEPTH, D, D_FF), jnp.float32) * D ** -0.5,
        "ffn_w2": nrm(ks[22], (DEPTH, D_FF, D), jnp.float32) * D_FF ** -0.5,
        "final_g": 1.0 + 0.02 * nrm(ks[23], (D,), jnp.float32),
    }


def reference(x, c, ctx, c_ctx, ada_w, ada_b, norm_mix_g, norm_ffn_g, s5_a_re, s5_a_im, s5_log_dt,
              s5_b_re, s5_b_im, s5_c_re, s5_c_im, s5_d, s5_w_glu, attn_w_qkv, attn_q_g, attn_k_g,
              attn_w_o, ffn_w1, ffn_w2, final_g):
    L = x.shape[1]
    rows = L // GRID_W
    row = jnp.repeat(jnp.arange(rows, dtype=jnp.float32), GRID_W)
    col = jnp.tile(jnp.arange(GRID_W, dtype=jnp.float32), rows)
    n_freq = ROPE_AXIS_DIM // 2
    inv_freq = ROPE_THETA ** (-jnp.arange(n_freq, dtype=jnp.float32) / n_freq)
    ang_row = row[:, None] * inv_freq[None, :]
    ang_col = col[:, None] * inv_freq[None, :]

    silu_c = jax.nn.silu(c)
    silu_cc = jax.nn.silu(c_ctx)
    for i in range(DEPTH):
        last = i == DEPTH - 1
        mod = (silu_c @ ada_w[i] + ada_b[i])[:, None, :]
        mod_c = (silu_cc @ ada_w[i] + ada_b[i])[None, None, :]
        sh1, sc1, g1, sh2, sc2, g2 = jnp.split(mod, 6, axis=-1)
        csh1, csc1, cg1, csh2, csc2, cg2 = jnp.split(mod_c, 6, axis=-1)

        h = _modulate(_rmsnorm(x, norm_mix_g[i]), sh1, sc1)
        hc = _modulate(_rmsnorm(ctx, norm_mix_g[i]), csh1, csc1)
        j = i // N_MIXERS
        if i % N_MIXERS == 0:
            y, yc = _s5_mixer(h, hc, s5_a_re[j], s5_a_im[j], s5_log_dt[j], s5_b_re[j], s5_b_im[j],
                              s5_c_re[j], s5_c_im[j], s5_d[j], s5_w_glu[j], not last)
        else:
            y, yc = _attn_mixer(h, hc, attn_w_qkv[j], attn_q_g[j], attn_k_g[j], attn_w_o[j],
                                ang_row, ang_col, not last)
        x = x + g1 * y
        h = _modulate(_rmsnorm(x, norm_ffn_g[i]), sh2, sc2)
        x = x + g2 * _sq_relu_mlp(h, ffn_w1[i], ffn_w2[i])
        if not last:
            ctx = ctx + cg1 * yc
            hc = _modulate(_rmsnorm(ctx, norm_ffn_g[i]), csh2, csc2)
            ctx = ctx + cg2 * _sq_relu_mlp(hc, ffn_w1[i], ffn_w2[i])
    return _rmsnorm(x, final_g)
```

```python
import functools
import math

import jax
import jax.numpy as jnp
from jax import lax
from jax.experimental import pallas as pl
from jax.experimental.pallas import tpu as pltpu

F32 = jnp.float32
BF16 = jnp.bfloat16

GRID_W = 64
N_HEADS = 16
N_KV_HEADS = 4
Q_PER_KV = N_HEADS // N_KV_HEADS
S5_GROUP = 16
ROPE_THETA = 10000.0
NORM_EPS = 1e-6

SUBLANES = 8
LANES = 128
MXU_DIM = 256

ROW_TILE = 256
SCAN_STEPS = 32
Q_TILE = 256
S5_CH_BLOCK = MXU_DIM
VMEM_LIMIT = 48 * 1024 * 1024


def _cparams(semantics):
    return pltpu.CompilerParams(dimension_semantics=semantics, vmem_limit_bytes=VMEM_LIMIT)


def _per_batch(x, fn):
    r, d = x.shape
    return fn(x.reshape(r // SUBLANES, SUBLANES, d)).reshape(r, d)


def _rms(x):
    return lax.rsqrt(jnp.mean(x * x, axis=-1, keepdims=True) + NORM_EPS)


def _norm_mod(x, g, shift, scale):
    h = (x * _rms(x)) * g
    return _per_batch(h, lambda h3: h3 * (1.0 + scale)[None] + shift[None])


def _to_stream_kernel(ctx_ref, x_ref, o_ref):
    j = pl.program_id(1)

    @pl.when(j == 0)
    def _():
        o_ref[...] = ctx_ref[...]

    @pl.when(j > 0)
    def _():
        o_ref[...] = x_ref[...]


def _to_stream(x, ctx):
    b, l, d = x.shape
    cl = ctx.shape[1]
    t = cl + l
    out = pl.pallas_call(
        _to_stream_kernel,
        grid=(b, t // cl),
        in_specs=[
            pl.BlockSpec((None, cl, d), lambda bi, j: (bi, 0, 0)),
            pl.BlockSpec((None, cl, d), lambda bi, j: (bi, jnp.maximum(j - 1, 0), 0)),
        ],
        out_specs=pl.BlockSpec((cl, d), lambda bi, j: (j, bi)),
        out_shape=jax.ShapeDtypeStruct((t, b * d), F32),
        compiler_params=_cparams(("arbitrary", "arbitrary")),
        name="to_stream",
    )(ctx, x)
    return out.reshape(t * b, d)


def _adaln_kernel(c_ref, w_ref, b_ref, o_ref):
    s = jax.nn.silu(c_ref[...]).astype(BF16)
    o_ref[...] = jnp.dot(s, w_ref[...].astype(BF16), preferred_element_type=F32) + b_ref[...]


def _adaln(c2, ada_w, ada_b):
    depth, d, n = ada_w.shape
    tn = n // 4
    return pl.pallas_call(
        _adaln_kernel,
        grid=(depth, n // tn),
        in_specs=[
            pl.BlockSpec(c2.shape, lambda i, j: (0, 0)),
            pl.BlockSpec((None, d, tn), lambda i, j: (i, 0, j)),
            pl.BlockSpec((None, 1, tn), lambda i, j: (i, 0, j)),
        ],
        out_specs=pl.BlockSpec((None, c2.shape[0], tn), lambda i, j: (i, 0, j)),
        out_shape=jax.ShapeDtypeStruct((depth, c2.shape[0], n), F32),
        compiler_params=_cparams(("arbitrary", "arbitrary")),
        name="adaln",
    )(c2, ada_w, ada_b.reshape(depth, 1, n))


def _s5_scan_kernel(x_ref, mod_ref, g_ref, d_ref, a_ref, bw_ref, cw_ref, y_ref, bu_ref, st_ref,
                    *, steps, n_blocks, ch_block, st_block):
    p = pl.program_id(0)
    k = pl.program_id(1)
    d_model = x_ref.shape[1]

    @pl.when(k == 0)
    def _():
        st_ref[...] = jnp.zeros_like(st_ref)

    mod = mod_ref[...]
    u = _norm_mod(x_ref[...], g_ref[...], mod[:, :d_model], mod[:, d_model:])
    ub = u.astype(BF16)

    for cb in range(n_blocks):
        bu_ref[:, cb * 2 * st_block:(cb + 1) * 2 * st_block] = jnp.dot(
            ub[:, cb * ch_block:(cb + 1) * ch_block], bw_ref[cb], preferred_element_type=F32)

    for cb in range(n_blocks):
        re0 = cb * 2 * st_block
        im0 = re0 + st_block
        ar = a_ref[0, :, cb * st_block:(cb + 1) * st_block]
        ai = a_ref[1, :, cb * st_block:(cb + 1) * st_block]

        def body(i, carry, re0=re0, im0=im0, ar=ar, ai=ai):
            hr, hi = carry
            t = i + p * (steps - 1 - 2 * i)
            row = pl.multiple_of(t * SUBLANES, SUBLANES)
            br = bu_ref[pl.ds(row, SUBLANES), re0:re0 + st_block]
            bi = bu_ref[pl.ds(row, SUBLANES), im0:im0 + st_block]
            nr = ar * hr - ai * hi + br
            ni = ar * hi + ai * hr + bi
            bu_ref[pl.ds(row, SUBLANES), re0:re0 + st_block] = nr
            bu_ref[pl.ds(row, SUBLANES), im0:im0 + st_block] = ni
            return nr, ni

        hr, hi = lax.fori_loop(
            0, steps, body,
            (st_ref[0, :, cb * st_block:(cb + 1) * st_block],
             st_ref[1, :, cb * st_block:(cb + 1) * st_block]),
            unroll=4)
        st_ref[0, :, cb * st_block:(cb + 1) * st_block] = hr
        st_ref[1, :, cb * st_block:(cb + 1) * st_block] = hi

    d_eff = d_ref[...] * jnp.where(p == 0, 1.0, 0.0)
    for cb in range(n_blocks):
        hb = bu_ref[:, cb * 2 * st_block:(cb + 1) * 2 * st_block].astype(BF16)
        ycb = jnp.dot(hb, cw_ref[cb], preferred_element_type=F32)
        cs = slice(cb * ch_block, (cb + 1) * ch_block)
        y_ref[:, cs] = ycb + d_eff[:, cs] * u[:, cs]


def _s5_scan(xs, mod, layer, g, dvec, abar, bw, cw, n_ctx_steps):
    rows, d = xs.shape
    n_blocks, ch_block, st2 = bw.shape[1:]
    st_block = st2 // 2
    tile = SCAN_STEPS * SUBLANES
    n_tiles = rows // tile
    n_ctx = n_ctx_steps // SCAN_STEPS

    def tile_of(p, k):
        bwd = jnp.where(k < n_ctx, n_ctx - 1 - k, n_tiles - 1 + n_ctx - k)
        return jnp.where(p == 0, k, bwd)

    def is_ctx(p, k):
        return jnp.where(tile_of(p, k) < n_ctx, 1, 0)

    kern = functools.partial(_s5_scan_kernel, steps=SCAN_STEPS, n_blocks=n_blocks,
                             ch_block=ch_block, st_block=st_block)
    return pl.pallas_call(
        kern,
        grid=(2, n_tiles),
        in_specs=[
            pl.BlockSpec((tile, d), lambda p, k: (tile_of(p, k), 0)),
            pl.BlockSpec((None, None, SUBLANES, 2 * d), lambda p, k: (layer, is_ctx(p, k), 0, 0)),
            pl.BlockSpec((1, d), lambda p, k: (0, 0)),
            pl.BlockSpec((1, d), lambda p, k: (0, 0)),
            pl.BlockSpec((None, 2, SUBLANES, abar.shape[-1]), lambda p, k: (p, 0, 0, 0)),
            pl.BlockSpec((None,) + bw.shape[1:], lambda p, k: (p, 0, 0, 0)),
            pl.BlockSpec((None,) + cw.shape[1:], lambda p, k: (p, 0, 0, 0)),
        ],
        out_specs=pl.BlockSpec((None, tile, d), lambda p, k: (p, tile_of(p, k), 0)),
        out_shape=jax.ShapeDtypeStruct((2, rows, d), F32),
        scratch_shapes=[
            pltpu.VMEM((tile, n_blocks * st2), F32),
            pltpu.VMEM((2, SUBLANES, n_blocks * st_block), F32),
        ],
        compiler_params=_cparams(("arbitrary", "arbitrary")),
        name="s5_scan",
    )(xs, mod, g, dvec, abar, bw, cw)


def _glu_kernel(yf_ref, yb_ref, x_ref, g1_ref, w_ref, o_ref):
    d_model = x_ref.shape[1]
    y = yf_ref[...] + yb_ref[...]
    z = jnp.dot(jax.nn.gelu(y).astype(BF16), w_ref[...], preferred_element_type=F32)
    glu = z[:, :d_model] * jax.nn.sigmoid(z[:, d_model:])
    gate = g1_ref[...]
    o_ref[...] = x_ref[...] + _per_batch(glu, lambda t: t * gate[None])


def _glu_out(y2, xs, mod, layer, w_glu, n_ctx_tiles):
    rows, d = xs.shape

    def is_ctx(i):
        return jnp.where(i < n_ctx_tiles, 1, 0)

    return pl.pallas_call(
        _glu_kernel,
        grid=(rows // ROW_TILE,),
        in_specs=[
            pl.BlockSpec((None, ROW_TILE, d), lambda i: (0, i, 0)),
            pl.BlockSpec((None, ROW_TILE, d), lambda i: (1, i, 0)),
            pl.BlockSpec((ROW_TILE, d), lambda i: (i, 0)),
            pl.BlockSpec((None, None, SUBLANES, d), lambda i: (layer, is_ctx(i), 0, 2)),
            pl.BlockSpec(w_glu.shape, lambda i: (0, 0)),
        ],
        out_specs=pl.BlockSpec((ROW_TILE, d), lambda i: (i, 0)),
        out_shape=jax.ShapeDtypeStruct((rows, d), F32),
        compiler_params=_cparams(("arbitrary",)),
        name="s5_glu_out",
    )(y2, y2, xs, mod, w_glu)


def _ffn_kernel(x_ref, mod_ref, g_ref, w1_ref, w2_ref, o_ref, *, ff_chunk):
    d_model = x_ref.shape[1]
    x = x_ref[...]
    mod = mod_ref[...]
    h = _norm_mod(x, g_ref[...], mod[:, :d_model], mod[:, d_model:2 * d_model]).astype(BF16)
    acc = jnp.zeros(x.shape, F32)
    for c in range(w1_ref.shape[1] // ff_chunk):
        cs = slice(c * ff_chunk, (c + 1) * ff_chunk)
        a = jnp.dot(h, w1_ref[:, cs], preferred_element_type=F32)
        a = jnp.square(jnp.maximum(a, 0.0)).astype(BF16)
        acc = acc + jnp.dot(a, w2_ref[cs, :], preferred_element_type=F32)
    gate = mod[:, 2 * d_model:]
    o_ref[...] = x + _per_batch(acc, lambda t: t * gate[None])


def _ffn(xs, mod, layer, g, w1, w2, n_ctx_tiles):
    rows, d = xs.shape

    def is_ctx(i):
        return jnp.where(i < n_ctx_tiles, 1, 0)

    return pl.pallas_call(
        functools.partial(_ffn_kernel, ff_chunk=d),
        grid=(rows // ROW_TILE,),
        in_specs=[
            pl.BlockSpec((ROW_TILE, d), lambda i: (i, 0)),
            pl.BlockSpec((None, None, SUBLANES, 3 * d), lambda i: (layer, is_ctx(i), 0, 1)),
            pl.BlockSpec((1, d), lambda i: (0, 0)),
            pl.BlockSpec(w1.shape, lambda i: (0, 0)),
            pl.BlockSpec(w2.shape, lambda i: (0, 0)),
        ],
        out_specs=pl.BlockSpec((ROW_TILE, d), lambda i: (i, 0)),
        out_shape=jax.ShapeDtypeStruct((rows, d), F32),
        compiler_params=_cparams(("arbitrary",)),
        name="ffn",
    )(xs, mod, g, w1, w2)


def _qkv_kernel(x_ref, mod_ref, g_ref, w_ref, gain_ref, cos_ref, sin_ref, q_ref, k_ref, v_ref,
                *, head_dim):
    d_model = x_ref.shape[1]
    mod = mod_ref[...]
    h = _norm_mod(x_ref[...], g_ref[...], mod[:, :d_model], mod[:, d_model:]).astype(BF16)
    qkv = jnp.dot(h, w_ref[...], preferred_element_type=F32)
    rows = qkv.shape[0]
    nq = q_ref.shape[1]
    nk = k_ref.shape[1]

    lane = lax.broadcasted_iota(jnp.int32, (rows, LANES), 1)
    low_head = lane < head_dim
    first_half = (lane & (head_dim // 4)) == 0
    cos = cos_ref[...]
    sin = sin_ref[...]

    def norm_rope(blk, gain):
        sq = blk * blk
        lo = jnp.sum(jnp.where(low_head, sq, 0.0), axis=-1, keepdims=True)
        hi = jnp.sum(jnp.where(low_head, 0.0, sq), axis=-1, keepdims=True)
        r = lax.rsqrt(jnp.where(low_head, lo, hi) * (1.0 / head_dim) + NORM_EPS)
        xn = (blk * r) * gain
        pair = jnp.where(first_half,
                         pltpu.roll(xn, LANES - head_dim // 4, axis=1),
                         pltpu.roll(xn, head_dim // 4, axis=1))
        return xn * cos + pair * sin

    for j in range(nq // LANES):
        cs = slice(j * LANES, (j + 1) * LANES)
        q_ref[:, cs] = norm_rope(qkv[:, cs], gain_ref[0:1, :]).astype(BF16)
    for j in range(nk // LANES):
        cs = slice(j * LANES, (j + 1) * LANES)
        k_ref[:, cs] = norm_rope(qkv[:, nq + j * LANES:nq + (j + 1) * LANES],
                                 gain_ref[1:2, :]).astype(BF16)
    v_ref[...] = qkv[:, nq + nk:].astype(BF16)


def _qkv(xs, mod, layer, g, w, gains, cos, sin, n_ctx_tiles, head_dim):
    rows, d = xs.shape
    nq = N_HEADS * head_dim
    nk = N_KV_HEADS * LANES

    def is_ctx(i):
        return jnp.where(i < n_ctx_tiles, 1, 0)

    return pl.pallas_call(
        functools.partial(_qkv_kernel, head_dim=head_dim),
        grid=(rows // ROW_TILE,),
        in_specs=[
            pl.BlockSpec((ROW_TILE, d), lambda i: (i, 0)),
            pl.BlockSpec((None, None, SUBLANES, 2 * d), lambda i: (layer, is_ctx(i), 0, 0)),
            pl.BlockSpec((1, d), lambda i: (0, 0)),
            pl.BlockSpec(w.shape, lambda i: (0, 0)),
            pl.BlockSpec(gains.shape, lambda i: (0, 0)),
            pl.BlockSpec((ROW_TILE, LANES), lambda i: (i, 0)),
            pl.BlockSpec((ROW_TILE, LANES), lambda i: (i, 0)),
        ],
        out_specs=[
            pl.BlockSpec((ROW_TILE, nq), lambda i: (i, 0)),
            pl.BlockSpec((ROW_TILE, nk), lambda i: (i, 0)),
            pl.BlockSpec((ROW_TILE, nk), lambda i: (i, 0)),
        ],
        out_shape=[
            jax.ShapeDtypeStruct((rows, nq), BF16),
            jax.ShapeDtypeStruct((rows, nk), BF16),
            jax.ShapeDtypeStruct((rows, nk), BF16),
        ],
        compiler_params=_cparams(("arbitrary",)),
        name="attn_qkv",
    )(xs, mod, g, w, gains, cos, sin)


def _attn_kernel(q_ref, k_ref, v_ref, o_ref, *, head_dim, n_ctx, first_tile):
    tq = q_ref.shape[0]
    lane = lax.broadcasted_iota(jnp.int32, (tq, LANES), 1)
    low_head = lane < head_dim

    def attend(n_keys):
        k = k_ref[0:n_keys, :]
        v = v_ref[0:n_keys, :]
        for j in range(q_ref.shape[1] // LANES):
            qblk = q_ref[:, j * LANES:(j + 1) * LANES].astype(F32)
            halves = []
            for keep in (low_head, jnp.logical_not(low_head)):
                qh = jnp.where(keep, qblk, 0.0).astype(BF16)
                s = lax.dot_general(qh, k, (((1,), (1,)), ((), ())), preferred_element_type=F32)
                m = jnp.max(s, axis=-1, keepdims=True)
                e = jnp.exp(s - m)
                den = jnp.sum(e, axis=-1, keepdims=True)
                o = jnp.dot(e.astype(BF16), v, preferred_element_type=F32)
                halves.append(o / den)
            o_ref[:, j * LANES:(j + 1) * LANES] = jnp.where(low_head, halves[0], halves[1]).astype(BF16)

    if first_tile == 0:
        @pl.when(pl.program_id(2) == 0)
        def _():
            attend(n_ctx)

        @pl.when(pl.program_id(2) > 0)
        def _():
            attend(k_ref.shape[0])
    else:
        attend(k_ref.shape[0])


def _attention(q, k2, v2, t_all, n_batch, n_ctx, first_tile, head_dim):
    qw = Q_PER_KV * head_dim
    q2 = q.reshape(t_all, n_batch * N_HEADS * head_dim)
    kk = k2.reshape(t_all, n_batch * N_KV_HEADS * LANES)
    vv = v2.reshape(t_all, n_batch * N_KV_HEADS * LANES)
    n_tiles = t_all // Q_TILE - first_tile
    out = pl.pallas_call(
        functools.partial(_attn_kernel, head_dim=head_dim, n_ctx=n_ctx, first_tile=first_tile),
        grid=(n_batch, N_KV_HEADS, n_tiles),
        in_specs=[
            pl.BlockSpec((Q_TILE, qw), lambda b, h, i: (i + first_tile, b * N_KV_HEADS + h)),
            pl.BlockSpec((t_all, LANES), lambda b, h, i: (0, b * N_KV_HEADS + h)),
            pl.BlockSpec((t_all, LANES), lambda b, h, i: (0, b * N_KV_HEADS + h)),
        ],
        out_specs=pl.BlockSpec((Q_TILE, qw), lambda b, h, i: (i, b * N_KV_HEADS + h)),
        out_shape=jax.ShapeDtypeStruct((n_tiles * Q_TILE, n_batch * N_HEADS * head_dim), BF16),
        compiler_params=_cparams(("arbitrary", "arbitrary", "arbitrary")),
        name="attention",
    )(q2, kk, vv)
    return out.reshape(n_tiles * Q_TILE * n_batch, N_HEADS * head_dim)


def _attn_out_kernel(o_ref, x_ref, g1_ref, w_ref, y_ref):
    y = jnp.dot(o_ref[...], w_ref[...], preferred_element_type=F32)
    gate = g1_ref[...]
    y_ref[...] = x_ref[...] + _per_batch(y, lambda t: t * gate[None])


def _attn_out(o, xs, mod, layer, w_o, first_row_tile, n_ctx_tiles):
    rows, d = o.shape

    def is_ctx(i):
        return jnp.where(i + first_row_tile < n_ctx_tiles, 1, 0)

    return pl.pallas_call(
        _attn_out_kernel,
        grid=(rows // ROW_TILE,),
        in_specs=[
            pl.BlockSpec((ROW_TILE, d), lambda i: (i, 0)),
            pl.BlockSpec((ROW_TILE, d), lambda i: (i + first_row_tile, 0)),
            pl.BlockSpec((None, None, SUBLANES, d), lambda i: (layer, is_ctx(i), 0, 2)),
            pl.BlockSpec(w_o.shape, lambda i: (0, 0)),
        ],
        out_specs=pl.BlockSpec((ROW_TILE, d), lambda i: (i, 0)),
        out_shape=jax.ShapeDtypeStruct((rows, d), F32),
        compiler_params=_cparams(("arbitrary",)),
        name="attn_out",
    )(o, xs, mod, w_o)


def _final_kernel(x_ref, g_ref, o_ref):
    x = x_ref[...]
    o_ref[...] = (x * _rms(x)) * g_ref[...]


def _final_norm(xs, g, n_batch, seq):
    d = xs.shape[1]
    x2 = xs.reshape(seq, n_batch * d)
    return pl.pallas_call(
        _final_kernel,
        grid=(n_batch, seq // ROW_TILE),
        in_specs=[
            pl.BlockSpec((ROW_TILE, d), lambda b, j: (j, b)),
            pl.BlockSpec((1, d), lambda b, j: (0, 0)),
        ],
        out_specs=pl.BlockSpec((None, ROW_TILE, d), lambda b, j: (b, j, 0)),
        out_shape=jax.ShapeDtypeStruct((n_batch, seq, d), F32),
        compiler_params=_cparams(("arbitrary", "arbitrary")),
        name="final_norm",
    )(x2, g)


def _block_diag(blocks, per_block):
    g, r, c = blocks.shape
    nb = g // per_block
    eye = jnp.eye(per_block, dtype=blocks.dtype)
    out = jnp.einsum('ngrc,gk->ngrkc', blocks.reshape(nb, per_block, r, c), eye)
    return out.reshape(nb, per_block * r, per_block * c)


def _s5_params(a_re, a_im, log_dt, b_re, b_im, c_re, c_im):
    per_block = S5_CH_BLOCK // S5_GROUP
    abars, bws, cws = [], [], []
    for dirn in range(2):
        ar = a_re[dirn].astype(F32)
        ai = a_im[dirn].astype(F32)
        dt = jnp.exp(log_dt[dirn].astype(F32))[:, None]
        mag = jnp.exp(dt * ar)
        abar_re = mag * jnp.cos(dt * ai)
        abar_im = mag * jnp.sin(dt * ai)
        den = ar * ar + ai * ai
        nr = abar_re - 1.0
        f_re = (nr * ar + abar_im * ai) / den
        f_im = (abar_im * ar - nr * ai) / den
        br = b_re[dirn].astype(F32)
        bi = b_im[dirn].astype(F32)
        bb_re = f_re[..., None] * br - f_im[..., None] * bi
        bb_im = f_re[..., None] * bi + f_im[..., None] * br
        bw = jnp.concatenate([_block_diag(jnp.swapaxes(bb_re, 1, 2), per_block),
                              _block_diag(jnp.swapaxes(bb_im, 1, 2), per_block)], axis=-1)
        cw = jnp.concatenate([_block_diag(jnp.swapaxes(c_re[dirn].astype(F32), 1, 2), per_block),
                              _block_diag(-jnp.swapaxes(c_im[dirn].astype(F32), 1, 2), per_block)],
                             axis=1)
        ab = jnp.stack([abar_re.reshape(-1), abar_im.reshape(-1)])
        abars.append(jnp.broadcast_to(ab[:, None, :], (2, SUBLANES, ab.shape[-1])))
        bws.append(bw.astype(BF16))
        cws.append(cw.astype(BF16))
    return jnp.stack(abars), jnp.stack(bws), jnp.stack(cws)


def _rope_tables(n_ctx, seq, n_batch, head_dim):
    axis_dim = head_dim // 2
    n_freq = axis_dim // 2
    pos = jnp.arange(seq, dtype=jnp.int32)
    row = (pos // GRID_W).astype(F32)
    col = (pos % GRID_W).astype(F32)
    inv_freq = ROPE_THETA ** (-jnp.arange(n_freq, dtype=F32) / n_freq)
    ang_row = row[:, None] * inv_freq[None, :]
    ang_col = col[:, None] * inv_freq[None, :]
    cos = jnp.concatenate([jnp.cos(ang_row)] * 2 + [jnp.cos(ang_col)] * 2, axis=-1)
    sin = jnp.concatenate([-jnp.sin(ang_row), jnp.sin(ang_row),
                           -jnp.sin(ang_col), jnp.sin(ang_col)], axis=-1)
    cos = jnp.concatenate([jnp.ones((n_ctx, head_dim), F32), cos], axis=0)
    sin = jnp.concatenate([jnp.zeros((n_ctx, head_dim), F32), sin], axis=0)
    per_block = LANES // head_dim

    def expand(t):
        t = jnp.tile(t, (1, per_block))
        return jnp.repeat(t, n_batch, axis=0)

    return expand(cos), expand(sin)


def _qkv_params(w_qkv, q_g, k_g, head_dim):
    d = w_qkv.shape[0]
    nq = N_HEADS * head_dim
    nkv = N_KV_HEADS * head_dim
    per_block = LANES // head_dim

    def dup(w):
        w = w.reshape(d, N_KV_HEADS, 1, head_dim)
        return jnp.broadcast_to(w, (d, N_KV_HEADS, per_block, head_dim)).reshape(d, -1)

    w = jnp.concatenate([w_qkv[:, :nq], dup(w_qkv[:, nq:nq + nkv]), dup(w_qkv[:, nq + nkv:])], axis=1)
    gains = jnp.stack([jnp.tile(q_g * (head_dim ** -0.5), per_block), jnp.tile(k_g, per_block)])
    return w.astype(BF16), gains


def kernel(x, c, ctx, c_ctx, ada_w, ada_b, norm_mix_g, norm_ffn_g, s5_a_re, s5_a_im, s5_log_dt,
           s5_b_re, s5_b_im, s5_c_re, s5_c_im, s5_d, s5_w_glu, attn_w_qkv, attn_q_g, attn_k_g,
           attn_w_o, ffn_w1, ffn_w2, final_g):
    n_batch, seq, d = x.shape
    n_ctx = ctx.shape[1]
    depth = ada_w.shape[0]
    head_dim = d // N_HEADS
    t_all = n_ctx + seq
    assert n_batch == SUBLANES and n_ctx == Q_TILE and seq % Q_TILE == 0
    assert n_ctx % SCAN_STEPS == 0 and seq % SCAN_STEPS == 0 and d % S5_CH_BLOCK == 0
    ctx_row_tiles = n_ctx * n_batch // ROW_TILE

    xs = _to_stream(x, ctx)
    c2 = jnp.concatenate([c, jnp.broadcast_to(c_ctx[None, :], (SUBLANES, d))], axis=0)
    mod = _adaln(c2, ada_w, ada_b).reshape(depth, 2, SUBLANES, 6 * d)
    cos, sin = _rope_tables(n_ctx, seq, n_batch, head_dim)

    for i in range(depth):
        last = i == depth - 1
        j = i // 2
        g_mix = norm_mix_g[i].reshape(1, d)
        g_ffn = norm_ffn_g[i].reshape(1, d)
        if i % 2 == 0:
            abar, bw, cw = _s5_params(s5_a_re[j], s5_a_im[j], s5_log_dt[j], s5_b_re[j], s5_b_im[j],
                                      s5_c_re[j], s5_c_im[j])
            y2 = _s5_scan(xs, mod, i, g_mix, s5_d[j].reshape(1, d), abar, bw, cw, n_ctx)
            xs = _glu_out(y2, xs, mod, i, s5_w_glu[j].astype(BF16), ctx_row_tiles)
            n_ctx_tiles = ctx_row_tiles
        else:
            w, gains = _qkv_params(attn_w_qkv[j], attn_q_g[j], attn_k_g[j], head_dim)
            q, k2, v2 = _qkv(xs, mod, i, g_mix, w, gains, cos, sin, ctx_row_tiles, head_dim)
            first_tile = 1 if last else 0
            o = _attention(q, k2, v2, t_all, n_batch, n_ctx, first_tile, head_dim)
            xs = _attn_out(o, xs, mod, i, attn_w_o[j].astype(BF16),
                           first_tile * ctx_row_tiles, ctx_row_tiles)
            n_ctx_tiles = 0 if last else ctx_row_tiles
        xs = _ffn(xs, mod, i, g_ffn, ffn_w1[i].astype(BF16), ffn_w2[i].astype(BF16), n_ctx_tiles)

    if xs.shape[0] != seq * n_batch:
        xs = xs[n_ctx * n_batch:]
    return _final_norm(xs, final_g.reshape(1, d), n_batch, seq)
```

```python
import functools

import numpy as np
import jax
import jax.numpy as jnp
from jax import lax
from jax.experimental import pallas as pl
from jax.experimental.pallas import tpu as pltpu

F32 = jnp.float32
BF16 = jnp.bfloat16

GRID_W = 64
N_HEADS = 16
N_KV_HEADS = 4
Q_PER_KV = N_HEADS // N_KV_HEADS
S5_GROUP = 16
ROPE_THETA = 10000.0
NORM_EPS = 1e-6

SUBLANES = 8
LANES = 128
MXU_DIM = 256

ROW_TILE = 256
SCAN_STEPS = 32
Q_TILE = 256
S5_CH_BLOCK = MXU_DIM
VMEM_LIMIT = 48 * 1024 * 1024


def _cparams(semantics):
    return pltpu.CompilerParams(dimension_semantics=semantics, vmem_limit_bytes=VMEM_LIMIT)


def _per_batch(v, rows):
    return jnp.tile(v, (rows // SUBLANES, 1))


def _rms(x):
    return lax.rsqrt(jnp.mean(x * x, axis=-1, keepdims=True) + NORM_EPS)


def _norm_mod(x, g, shift, scale):
    h = (x * _rms(x)) * g
    return h * _per_batch(1.0 + scale, x.shape[0]) + _per_batch(shift, x.shape[0])


def _row_perm(rows, to_batch_major):
    nt = rows // SUBLANES
    r = np.arange(rows)
    if to_batch_major:
        src = (r % nt) * SUBLANES + r // nt
    else:
        src = (r % SUBLANES) * nt + r // SUBLANES
    return jnp.asarray(np.equal(src[:, None], r[None, :]), dtype=BF16)


def _permute_exact(pm, x):
    hi = x.astype(BF16)
    r1 = x - hi.astype(F32)
    mid = r1.astype(BF16)
    lo = (r1 - mid.astype(F32)).astype(BF16)
    return (jnp.dot(pm, hi, preferred_element_type=F32)
            + jnp.dot(pm, mid, preferred_element_type=F32)
            + jnp.dot(pm, lo, preferred_element_type=F32))


def _to_stream_kernel(pm_ref, ctx_ref, x_ref, o_ref, *, n_ctx_tiles):
    j = pl.program_id(0)
    rows = o_ref.shape[0]

    @pl.when(j < n_ctx_tiles)
    def _():
        o_ref[...] = _permute_exact(pm_ref[...], ctx_ref[...].reshape(rows, -1))

    @pl.when(j >= n_ctx_tiles)
    def _():
        o_ref[...] = _permute_exact(pm_ref[...], x_ref[...].reshape(rows, -1))


def _to_stream(x, ctx):
    b, l, d = x.shape
    cl = ctx.shape[1]
    steps = ROW_TILE // b
    n_ctx_tiles = cl // steps
    pm = _row_perm(ROW_TILE, to_batch_major=False)
    return pl.pallas_call(
        functools.partial(_to_stream_kernel, n_ctx_tiles=n_ctx_tiles),
        grid=((cl + l) // steps,),
        in_specs=[
            pl.BlockSpec(pm.shape, lambda j: (0, 0)),
            pl.BlockSpec((b, steps, d), lambda j: (0, jnp.minimum(j, n_ctx_tiles - 1), 0)),
            pl.BlockSpec((b, steps, d), lambda j: (0, jnp.maximum(j - n_ctx_tiles, 0), 0)),
        ],
        out_specs=pl.BlockSpec((ROW_TILE, d), lambda j: (j, 0)),
        out_shape=jax.ShapeDtypeStruct(((cl + l) * b, d), F32),
        compiler_params=_cparams(("arbitrary",)),
        name="to_stream",
    )(pm, ctx, x)


def _adaln_kernel(c_ref, w_ref, b_ref, o_ref):
    s = jax.nn.silu(c_ref[...]).astype(BF16)
    o_ref[...] = jnp.dot(s, w_ref[...].astype(BF16), preferred_element_type=F32) + b_ref[...]


def _adaln(c2, ada_w, ada_b):
    depth, d, n = ada_w.shape
    tn = n // 4
    return pl.pallas_call(
        _adaln_kernel,
        grid=(depth, n // tn),
        in_specs=[
            pl.BlockSpec(c2.shape, lambda i, j: (0, 0)),
            pl.BlockSpec((None, d, tn), lambda i, j: (i, 0, j)),
            pl.BlockSpec((None, 1, tn), lambda i, j: (i, 0, j)),
        ],
        out_specs=pl.BlockSpec((None, c2.shape[0], tn), lambda i, j: (i, 0, j)),
        out_shape=jax.ShapeDtypeStruct((depth, c2.shape[0], n), F32),
        compiler_params=_cparams(("arbitrary", "arbitrary")),
        name="adaln",
    )(c2, ada_w, ada_b.reshape(depth, 1, n))


def _s5_scan_kernel(x_ref, mod_ref, g_ref, d_ref, a_ref, bw_ref, cw_ref, y_ref, bu_ref, st_ref,
                    *, steps, n_blocks, ch_block, st_block):
    p = pl.program_id(0)
    k = pl.program_id(1)
    d_model = x_ref.shape[1]

    @pl.when(k == 0)
    def _():
        st_ref[...] = jnp.zeros_like(st_ref)

    mod = mod_ref[...]
    u = _norm_mod(x_ref[...], g_ref[...], mod[:, :d_model], mod[:, d_model:])
    ub = u.astype(BF16)

    for cb in range(n_blocks):
        bu_ref[:, cb * 2 * st_block:(cb + 1) * 2 * st_block] = jnp.dot(
            ub[:, cb * ch_block:(cb + 1) * ch_block], bw_ref[cb], preferred_element_type=F32)

    for cb in range(n_blocks):
        re0 = cb * 2 * st_block
        im0 = re0 + st_block
        ar = a_ref[0, :, cb * st_block:(cb + 1) * st_block]
        ai = a_ref[1, :, cb * st_block:(cb + 1) * st_block]

        def body(i, carry, re0=re0, im0=im0, ar=ar, ai=ai):
            hr, hi = carry
            t = i + p * (steps - 1 - 2 * i)
            row = pl.multiple_of(t * SUBLANES, SUBLANES)
            br = bu_ref[pl.ds(row, SUBLANES), re0:re0 + st_block]
            bi = bu_ref[pl.ds(row, SUBLANES), im0:im0 + st_block]
            nr = ar * hr - ai * hi + br
            ni = ar * hi + ai * hr + bi
            bu_ref[pl.ds(row, SUBLANES), re0:re0 + st_block] = nr
            bu_ref[pl.ds(row, SUBLANES), im0:im0 + st_block] = ni
            return nr, ni

        hr, hi = lax.fori_loop(
            0, steps, body,
            (st_ref[0, :, cb * st_block:(cb + 1) * st_block],
             st_ref[1, :, cb * st_block:(cb + 1) * st_block]),
            unroll=4)
        st_ref[0, :, cb * st_block:(cb + 1) * st_block] = hr
        st_ref[1, :, cb * st_block:(cb + 1) * st_block] = hi

    d_eff = d_ref[...] * jnp.where(p == 0, 1.0, 0.0)
    for cb in range(n_blocks):
        hb = bu_ref[:, cb * 2 * st_block:(cb + 1) * 2 * st_block].astype(BF16)
        ycb = jnp.dot(hb, cw_ref[cb], preferred_element_type=F32)
        cs = slice(cb * ch_block, (cb + 1) * ch_block)
        y_ref[:, cs] = ycb + d_eff[:, cs] * u[:, cs]


def _s5_scan(xs, mod, layer, g, dvec, abar, bw, cw, n_ctx_steps):
    rows, d = xs.shape
    n_blocks, ch_block, st2 = bw.shape[1:]
    st_block = st2 // 2
    tile = SCAN_STEPS * SUBLANES
    n_tiles = rows // tile
    n_ctx = n_ctx_steps // SCAN_STEPS

    def tile_of(p, k):
        bwd = jnp.where(k < n_ctx, n_ctx - 1 - k, n_tiles - 1 + n_ctx - k)
        return jnp.where(p == 0, k, bwd)

    def is_ctx(p, k):
        return jnp.where(tile_of(p, k) < n_ctx, 1, 0)

    kern = functools.partial(_s5_scan_kernel, steps=SCAN_STEPS, n_blocks=n_blocks,
                             ch_block=ch_block, st_block=st_block)
    return pl.pallas_call(
        kern,
        grid=(2, n_tiles),
        in_specs=[
            pl.BlockSpec((tile, d), lambda p, k: (tile_of(p, k), 0)),
            pl.BlockSpec((None, None, SUBLANES, 2 * d), lambda p, k: (layer, is_ctx(p, k), 0, 0)),
            pl.BlockSpec((1, d), lambda p, k: (0, 0)),
            pl.BlockSpec((1, d), lambda p, k: (0, 0)),
            pl.BlockSpec((None, 2, SUBLANES, abar.shape[-1]), lambda p, k: (p, 0, 0, 0)),
            pl.BlockSpec((None,) + bw.shape[1:], lambda p, k: (p, 0, 0, 0)),
            pl.BlockSpec((None,) + cw.shape[1:], lambda p, k: (p, 0, 0, 0)),
        ],
        out_specs=pl.BlockSpec((None, tile, d), lambda p, k: (p, tile_of(p, k), 0)),
        out_shape=jax.ShapeDtypeStruct((2, rows, d), F32),
        scratch_shapes=[
            pltpu.VMEM((tile, n_blocks * st2), F32),
            pltpu.VMEM((2, SUBLANES, n_blocks * st_block), F32),
        ],
        compiler_params=_cparams(("arbitrary", "arbitrary")),
        name="s5_scan",
    )(xs, mod, g, dvec, abar, bw, cw)


def _glu_kernel(yf_ref, yb_ref, x_ref, g1_ref, w_ref, o_ref):
    d_model = x_ref.shape[1]
    y = yf_ref[...] + yb_ref[...]
    z = jnp.dot(jax.nn.gelu(y).astype(BF16), w_ref[...], preferred_element_type=F32)
    glu = z[:, :d_model] * jax.nn.sigmoid(z[:, d_model:])
    o_ref[...] = x_ref[...] + _per_batch(g1_ref[...], glu.shape[0]) * glu


def _glu_out(y2, xs, mod, layer, w_glu, n_ctx_tiles):
    rows, d = xs.shape

    def is_ctx(i):
        return jnp.where(i < n_ctx_tiles, 1, 0)

    return pl.pallas_call(
        _glu_kernel,
        grid=(rows // ROW_TILE,),
        in_specs=[
            pl.BlockSpec((None, ROW_TILE, d), lambda i: (0, i, 0)),
            pl.BlockSpec((None, ROW_TILE, d), lambda i: (1, i, 0)),
            pl.BlockSpec((ROW_TILE, d), lambda i: (i, 0)),
            pl.BlockSpec((None, None, SUBLANES, d), lambda i: (layer, is_ctx(i), 0, 2)),
            pl.BlockSpec(w_glu.shape, lambda i: (0, 0)),
        ],
        out_specs=pl.BlockSpec((ROW_TILE, d), lambda i: (i, 0)),
        out_shape=jax.ShapeDtypeStruct((rows, d), F32),
        compiler_params=_cparams(("arbitrary",)),
        name="s5_glu_out",
    )(y2, y2, xs, mod, w_glu)


def _ffn_kernel(x_ref, mod_ref, g_ref, w1_ref, w2_ref, o_ref, *, ff_chunk):
    d_model = x_ref.shape[1]
    x = x_ref[...]
    mod = mod_ref[...]
    h = _norm_mod(x, g_ref[...], mod[:, :d_model], mod[:, d_model:2 * d_model]).astype(BF16)
    acc = jnp.zeros(x.shape, F32)
    for c in range(w1_ref.shape[1] // ff_chunk):
        cs = slice(c * ff_chunk, (c + 1) * ff_chunk)
        a = jnp.dot(h, w1_ref[:, cs], preferred_element_type=F32)
        a = jnp.square(jnp.maximum(a, 0.0)).astype(BF16)
        acc = acc + jnp.dot(a, w2_ref[cs, :], preferred_element_type=F32)
    o_ref[...] = x + _per_batch(mod[:, 2 * d_model:], x.shape[0]) * acc


def _ffn(xs, mod, layer, g, w1, w2, n_ctx_tiles):
    rows, d = xs.shape

    def is_ctx(i):
        return jnp.where(i < n_ctx_tiles, 1, 0)

    return pl.pallas_call(
        functools.partial(_ffn_kernel, ff_chunk=d),
        grid=(rows // ROW_TILE,),
        in_specs=[
            pl.BlockSpec((ROW_TILE, d), lambda i: (i, 0)),
            pl.BlockSpec((None, None, SUBLANES, 3 * d), lambda i: (layer, is_ctx(i), 0, 1)),
            pl.BlockSpec((1, d), lambda i: (0, 0)),
            pl.BlockSpec(w1.shape, lambda i: (0, 0)),
            pl.BlockSpec(w2.shape, lambda i: (0, 0)),
        ],
        out_specs=pl.BlockSpec((ROW_TILE, d), lambda i: (i, 0)),
        out_shape=jax.ShapeDtypeStruct((rows, d), F32),
        compiler_params=_cparams(("arbitrary",)),
        name="ffn",
    )(xs, mod, g, w1, w2)


def _qkv_kernel(x_ref, mod_ref, g_ref, pm_ref, w_ref, gain_ref, cos_ref, sin_ref,
                q_ref, k_ref, v_ref, *, head_dim):
    d_model = x_ref.shape[1]
    n_batch, steps, nq = q_ref.shape
    nk = k_ref.shape[2]
    mod = mod_ref[...]
    h = _norm_mod(x_ref[...], g_ref[...], mod[:, :d_model], mod[:, d_model:]).astype(BF16)
    h = jnp.dot(pm_ref[...], h, preferred_element_type=F32).astype(BF16)
    qkv = jnp.dot(h, w_ref[...], preferred_element_type=F32)
    rows = qkv.shape[0]

    lane = lax.broadcasted_iota(jnp.int32, (rows, LANES), 1)
    low_head = lane < head_dim
    first_half = (lane & (head_dim // 4)) == 0
    cos = jnp.tile(cos_ref[...], (n_batch, 1))
    sin = jnp.tile(sin_ref[...], (n_batch, 1))

    def put(ref, cs, val):
        for b in range(n_batch):
            ref[b, :, cs] = val[b * steps:(b + 1) * steps]

    def norm_rope(blk, gain):
        sq = blk * blk
        lo = jnp.sum(jnp.where(low_head, sq, 0.0), axis=-1, keepdims=True)
        hi = jnp.sum(jnp.where(low_head, 0.0, sq), axis=-1, keepdims=True)
        r = lax.rsqrt(jnp.where(low_head, lo, hi) * (1.0 / head_dim) + NORM_EPS)
        xn = (blk * r) * gain
        pair = jnp.where(first_half,
                         pltpu.roll(xn, LANES - head_dim // 4, axis=1),
                         pltpu.roll(xn, head_dim // 4, axis=1))
        return xn * cos + pair * sin

    for j in range(nq // LANES):
        cs = slice(j * LANES, (j + 1) * LANES)
        put(q_ref, cs, norm_rope(qkv[:, cs], gain_ref[0:1, :]).astype(BF16))
    for j in range(nk // LANES):
        cs = slice(j * LANES, (j + 1) * LANES)
        put(k_ref, cs, norm_rope(qkv[:, nq + j * LANES:nq + (j + 1) * LANES],
                                 gain_ref[1:2, :]).astype(BF16))
    for j in range(nk // LANES):
        cs = slice(j * LANES, (j + 1) * LANES)
        put(v_ref, cs, qkv[:, nq + nk + j * LANES:nq + nk + (j + 1) * LANES].astype(BF16))


def _qkv(xs, mod, layer, g, w, gains, cos, sin, n_ctx_tiles, head_dim):
    rows, d = xs.shape
    nq = N_HEADS * head_dim
    nk = N_KV_HEADS * LANES
    steps = ROW_TILE // SUBLANES
    t_all = rows // SUBLANES
    pm = _row_perm(ROW_TILE, to_batch_major=True)

    def is_ctx(i):
        return jnp.where(i < n_ctx_tiles, 1, 0)

    return pl.pallas_call(
        functools.partial(_qkv_kernel, head_dim=head_dim),
        grid=(rows // ROW_TILE,),
        in_specs=[
            pl.BlockSpec((ROW_TILE, d), lambda i: (i, 0)),
            pl.BlockSpec((None, None, SUBLANES, 2 * d), lambda i: (layer, is_ctx(i), 0, 0)),
            pl.BlockSpec((1, d), lambda i: (0, 0)),
            pl.BlockSpec(pm.shape, lambda i: (0, 0)),
            pl.BlockSpec(w.shape, lambda i: (0, 0)),
            pl.BlockSpec(gains.shape, lambda i: (0, 0)),
            pl.BlockSpec((steps, LANES), lambda i: (i, 0)),
            pl.BlockSpec((steps, LANES), lambda i: (i, 0)),
        ],
        out_specs=[
            pl.BlockSpec((SUBLANES, steps, nq), lambda i: (0, i, 0)),
            pl.BlockSpec((SUBLANES, steps, nk), lambda i: (0, i, 0)),
            pl.BlockSpec((SUBLANES, steps, nk), lambda i: (0, i, 0)),
        ],
        out_shape=[
            jax.ShapeDtypeStruct((SUBLANES, t_all, nq), BF16),
            jax.ShapeDtypeStruct((SUBLANES, t_all, nk), BF16),
            jax.ShapeDtypeStruct((SUBLANES, t_all, nk), BF16),
        ],
        compiler_params=_cparams(("arbitrary",)),
        name="attn_qkv",
    )(xs, mod, g, pm, w, gains, cos, sin)


def _attn_kernel(q_ref, k_ref, v_ref, o_ref, *, head_dim, n_ctx, first_tile):
    tq = q_ref.shape[0]
    lane = lax.broadcasted_iota(jnp.int32, (tq, LANES), 1)
    low_head = lane < head_dim

    def attend(n_keys):
        k = k_ref[0:n_keys, :]
        v = v_ref[0:n_keys, :]
        for j in range(q_ref.shape[1] // LANES):
            qblk = q_ref[:, j * LANES:(j + 1) * LANES].astype(F32)
            halves = []
            for keep in (low_head, jnp.logical_not(low_head)):
                qh = jnp.where(keep, qblk, 0.0).astype(BF16)
                s = lax.dot_general(qh, k, (((1,), (1,)), ((), ())), preferred_element_type=F32)
                m = jnp.max(s, axis=-1, keepdims=True)
                e = jnp.exp(s - m)
                den = jnp.sum(e, axis=-1, keepdims=True)
                o = jnp.dot(e.astype(BF16), v, preferred_element_type=F32)
                halves.append(o / den)
            o_ref[:, j * LANES:(j + 1) * LANES] = jnp.where(low_head, halves[0], halves[1]).astype(BF16)

    if first_tile == 0:
        @pl.when(pl.program_id(2) == 0)
        def _():
            attend(n_ctx)

        @pl.when(pl.program_id(2) > 0)
        def _():
            attend(k_ref.shape[0])
    else:
        attend(k_ref.shape[0])


def _attention(q, k2, v2, n_ctx, first_tile, head_dim):
    n_batch, t_all, nq = q.shape
    qw = Q_PER_KV * head_dim
    n_tiles = t_all // Q_TILE - first_tile
    return pl.pallas_call(
        functools.partial(_attn_kernel, head_dim=head_dim, n_ctx=n_ctx, first_tile=first_tile),
        grid=(n_batch, N_KV_HEADS, n_tiles),
        in_specs=[
            pl.BlockSpec((None, Q_TILE, qw), lambda b, h, i: (b, i + first_tile, h)),
            pl.BlockSpec((None, t_all, LANES), lambda b, h, i: (b, 0, h)),
            pl.BlockSpec((None, t_all, LANES), lambda b, h, i: (b, 0, h)),
        ],
        out_specs=pl.BlockSpec((None, Q_TILE, qw), lambda b, h, i: (b, i, h)),
        out_shape=jax.ShapeDtypeStruct((n_batch, n_tiles * Q_TILE, nq), BF16),
        compiler_params=_cparams(("arbitrary", "arbitrary", "arbitrary")),
        name="attention",
    )(q, k2, v2)


def _attn_out_kernel(o_ref, x_ref, g1_ref, pm_ref, w_ref, y_ref):
    rows = x_ref.shape[0]
    o = o_ref[...].reshape(rows, -1)
    o = jnp.dot(pm_ref[...], o, preferred_element_type=F32).astype(BF16)
    y = jnp.dot(o, w_ref[...], preferred_element_type=F32)
    y_ref[...] = x_ref[...] + _per_batch(g1_ref[...], rows) * y


def _attn_out(o, xs, mod, layer, w_o, first_row_tile, n_ctx_tiles):
    n_batch, n, d = o.shape
    steps = ROW_TILE // n_batch
    pm = _row_perm(ROW_TILE, to_batch_major=False)

    def is_ctx(i):
        return jnp.where(i + first_row_tile < n_ctx_tiles, 1, 0)

    return pl.pallas_call(
        _attn_out_kernel,
        grid=(n // steps,),
        in_specs=[
            pl.BlockSpec((n_batch, steps, d), lambda i: (0, i, 0)),
            pl.BlockSpec((ROW_TILE, d), lambda i: (i + first_row_tile, 0)),
            pl.BlockSpec((None, None, SUBLANES, d), lambda i: (layer, is_ctx(i), 0, 2)),
            pl.BlockSpec(pm.shape, lambda i: (0, 0)),
            pl.BlockSpec(w_o.shape, lambda i: (0, 0)),
        ],
        out_specs=pl.BlockSpec((ROW_TILE, d), lambda i: (i, 0)),
        out_shape=jax.ShapeDtypeStruct((n * n_batch, d), F32),
        compiler_params=_cparams(("arbitrary",)),
        name="attn_out",
    )(o, xs, mod, pm, w_o)


def _final_kernel(x_ref, g_ref, pm_ref, o_ref):
    x = x_ref[...]
    y = (x * _rms(x)) * g_ref[...]
    o_ref[...] = _permute_exact(pm_ref[...], y).reshape(o_ref.shape)


def _final_norm(xs, g, n_batch, seq, first_row_tile):
    d = xs.shape[1]
    steps = ROW_TILE // n_batch
    pm = _row_perm(ROW_TILE, to_batch_major=True)
    return pl.pallas_call(
        _final_kernel,
        grid=(seq // steps,),
        in_specs=[
            pl.BlockSpec((ROW_TILE, d), lambda j: (j + first_row_tile, 0)),
            pl.BlockSpec((1, d), lambda j: (0, 0)),
            pl.BlockSpec(pm.shape, lambda j: (0, 0)),
        ],
        out_specs=pl.BlockSpec((n_batch, steps, d), lambda j: (0, j, 0)),
        out_shape=jax.ShapeDtypeStruct((n_batch, seq, d), F32),
        compiler_params=_cparams(("arbitrary",)),
        name="final_norm",
    )(xs, g, pm)


def _block_diag(blocks, per_block):
    g, r, c = blocks.shape
    nb = g // per_block
    eye = jnp.eye(per_block, dtype=blocks.dtype)
    out = jnp.einsum('ngrc,gk->ngrkc', blocks.reshape(nb, per_block, r, c), eye)
    return out.reshape(nb, per_block * r, per_block * c)


def _s5_params(a_re, a_im, log_dt, b_re, b_im, c_re, c_im):
    per_block = S5_CH_BLOCK // S5_GROUP
    abars, bws, cws = [], [], []
    for dirn in range(2):
        ar = a_re[dirn].astype(F32)
        ai = a_im[dirn].astype(F32)
        dt = jnp.exp(log_dt[dirn].astype(F32))[:, None]
        mag = jnp.exp(dt * ar)
        abar_re = mag * jnp.cos(dt * ai)
        abar_im = mag * jnp.sin(dt * ai)
        den = ar * ar + ai * ai
        nr = abar_re - 1.0
        f_re = (nr * ar + abar_im * ai) / den
        f_im = (abar_im * ar - nr * ai) / den
        br = b_re[dirn].astype(F32)
        bi = b_im[dirn].astype(F32)
        bb_re = f_re[..., None] * br - f_im[..., None] * bi
        bb_im = f_re[..., None] * bi + f_im[..., None] * br
        bw = jnp.concatenate([_block_diag(jnp.swapaxes(bb_re, 1, 2), per_block),
                              _block_diag(jnp.swapaxes(bb_im, 1, 2), per_block)], axis=-1)
        cw = jnp.concatenate([_block_diag(jnp.swapaxes(c_re[dirn].astype(F32), 1, 2), per_block),
                              _block_diag(-jnp.swapaxes(c_im[dirn].astype(F32), 1, 2), per_block)],
                             axis=1)
        ab = jnp.stack([abar_re.reshape(-1), abar_im.reshape(-1)])
        abars.append(jnp.broadcast_to(ab[:, None, :], (2, SUBLANES, ab.shape[-1])))
        bws.append(bw.astype(BF16))
        cws.append(cw.astype(BF16))
    return jnp.stack(abars), jnp.stack(bws), jnp.stack(cws)


def _rope_tables(n_ctx, seq, head_dim):
    axis_dim = head_dim // 2
    n_freq = axis_dim // 2
    pos = jnp.arange(seq, dtype=jnp.int32)
    row = (pos // GRID_W).astype(F32)
    col = (pos % GRID_W).astype(F32)
    inv_freq = ROPE_THETA ** (-jnp.arange(n_freq, dtype=F32) / n_freq)
    ang_row = row[:, None] * inv_freq[None, :]
    ang_col = col[:, None] * inv_freq[None, :]
    cos = jnp.concatenate([jnp.cos(ang_row)] * 2 + [jnp.cos(ang_col)] * 2, axis=-1)
    sin = jnp.concatenate([-jnp.sin(ang_row), jnp.sin(ang_row),
                           -jnp.sin(ang_col), jnp.sin(ang_col)], axis=-1)
    cos = jnp.concatenate([jnp.ones((n_ctx, head_dim), F32), cos], axis=0)
    sin = jnp.concatenate([jnp.zeros((n_ctx, head_dim), F32), sin], axis=0)
    per_block = LANES // head_dim
    return jnp.tile(cos, (1, per_block)), jnp.tile(sin, (1, per_block))


def _qkv_params(w_qkv, q_g, k_g, head_dim):
    d = w_qkv.shape[0]
    nq = N_HEADS * head_dim
    nkv = N_KV_HEADS * head_dim
    per_block = LANES // head_dim

    def dup(w):
        w = w.reshape(d, N_KV_HEADS, 1, head_dim)
        return jnp.broadcast_to(w, (d, N_KV_HEADS, per_block, head_dim)).reshape(d, -1)

    w = jnp.concatenate([w_qkv[:, :nq], dup(w_qkv[:, nq:nq + nkv]), dup(w_qkv[:, nq + nkv:])], axis=1)
    gains = jnp.stack([jnp.tile(q_g * (head_dim ** -0.5), per_block), jnp.tile(k_g, per_block)])
    return w.astype(BF16), gains


def kernel(x, c, ctx, c_ctx, ada_w, ada_b, norm_mix_g, norm_ffn_g, s5_a_re, s5_a_im, s5_log_dt,
           s5_b_re, s5_b_im, s5_c_re, s5_c_im, s5_d, s5_w_glu, attn_w_qkv, attn_q_g, attn_k_g,
           attn_w_o, ffn_w1, ffn_w2, final_g):
    n_batch, seq, d = x.shape
    n_ctx = ctx.shape[1]
    depth = ada_w.shape[0]
    head_dim = d // N_HEADS
    t_all = n_ctx + seq
    assert n_batch == SUBLANES and n_ctx == Q_TILE and seq % Q_TILE == 0
    assert n_ctx % SCAN_STEPS == 0 and seq % SCAN_STEPS == 0 and d % S5_CH_BLOCK == 0
    ctx_row_tiles = n_ctx * n_batch // ROW_TILE

    xs = _to_stream(x, ctx)
    c2 = jnp.concatenate([c, jnp.broadcast_to(c_ctx[None, :], (SUBLANES, d))], axis=0)
    mod = _adaln(c2, ada_w, ada_b).reshape(depth, 2, SUBLANES, 6 * d)
    cos, sin = _rope_tables(n_ctx, seq, head_dim)
    has_ctx = True

    for i in range(depth):
        last = i == depth - 1
        j = i // 2
        g_mix = norm_mix_g[i].reshape(1, d)
        g_ffn = norm_ffn_g[i].reshape(1, d)
        if i % 2 == 0:
            abar, bw, cw = _s5_params(s5_a_re[j], s5_a_im[j], s5_log_dt[j], s5_b_re[j], s5_b_im[j],
                                      s5_c_re[j], s5_c_im[j])
            y2 = _s5_scan(xs, mod, i, g_mix, s5_d[j].reshape(1, d), abar, bw, cw, n_ctx)
            xs = _glu_out(y2, xs, mod, i, s5_w_glu[j].astype(BF16), ctx_row_tiles)
            n_ctx_tiles = ctx_row_tiles
        else:
            w, gains = _qkv_params(attn_w_qkv[j], attn_q_g[j], attn_k_g[j], head_dim)
            q, k2, v2 = _qkv(xs, mod, i, g_mix, w, gains, cos, sin, ctx_row_tiles, head_dim)
            first_tile = 1 if last else 0
            o = _attention(q, k2, v2, n_ctx, first_tile, head_dim)
            xs = _attn_out(o, xs, mod, i, attn_w_o[j].astype(BF16),
                           first_tile * ctx_row_tiles, ctx_row_tiles)
            has_ctx = not last
            n_ctx_tiles = ctx_row_tiles if has_ctx else 0
        xs = _ffn(xs, mod, i, g_ffn, ffn_w1[i].astype(BF16), ffn_w2[i].astype(BF16), n_ctx_tiles)

    return _final_norm(xs, final_g.reshape(1, d), n_batch, seq, ctx_row_tiles if has_ctx else 0)
```

```python
import functools

import numpy as np
import jax
import jax.numpy as jnp
from jax import lax
from jax.experimental import pallas as pl
from jax.experimental.pallas import tpu as pltpu

F32 = jnp.float32
BF16 = jnp.bfloat16

GRID_W = 64
N_HEADS = 16
N_KV_HEADS = 4
Q_PER_KV = N_HEADS // N_KV_HEADS
S5_GROUP = 16
ROPE_THETA = 10000.0
NORM_EPS = 1e-6
LOG2_E = 1.4426950408889634

SUBLANES = 8
LANES = 128
MXU_DIM = 256

ROW_TILE = 256
SCAN_STEPS = 32
Q_TILE = 256
S5_IN_BLOCK = LANES
S5_OUT_BLOCK = LANES
VMEM_LIMIT = 48 * 1024 * 1024


def _cparams(semantics):
    return pltpu.CompilerParams(dimension_semantics=semantics, vmem_limit_bytes=VMEM_LIMIT)


def _per_batch(v, rows):
    return jnp.tile(v, (rows // SUBLANES, 1))


def _rms(x):
    return lax.rsqrt(jnp.mean(x * x, axis=-1, keepdims=True) + NORM_EPS)


def _norm_mod(x, g, shift, scale):
    h = (x * _rms(x)) * g
    return h * _per_batch(1.0 + scale, x.shape[0]) + _per_batch(shift, x.shape[0])


def _row_perm(rows, to_batch_major):
    nt = rows // SUBLANES
    r = np.arange(rows)
    if to_batch_major:
        src = (r % nt) * SUBLANES + r // nt
    else:
        src = (r % SUBLANES) * nt + r // SUBLANES
    return jnp.asarray(np.equal(src[:, None], r[None, :]), dtype=BF16)


def _permute_exact(pm, x):
    hi = x.astype(BF16)
    r1 = x - hi.astype(F32)
    mid = r1.astype(BF16)
    lo = (r1 - mid.astype(F32)).astype(BF16)
    return (jnp.dot(pm, hi, preferred_element_type=F32)
            + jnp.dot(pm, mid, preferred_element_type=F32)
            + jnp.dot(pm, lo, preferred_element_type=F32))


def _to_stream_kernel(pm_ref, ctx_ref, x_ref, o_ref, *, n_ctx_tiles):
    j = pl.program_id(0)
    rows = o_ref.shape[0]

    @pl.when(j < n_ctx_tiles)
    def _():
        o_ref[...] = _permute_exact(pm_ref[...], ctx_ref[...].reshape(rows, -1))

    @pl.when(j >= n_ctx_tiles)
    def _():
        o_ref[...] = _permute_exact(pm_ref[...], x_ref[...].reshape(rows, -1))


def _to_stream(x, ctx):
    b, l, d = x.shape
    cl = ctx.shape[1]
    steps = ROW_TILE // b
    n_ctx_tiles = cl // steps
    pm = _row_perm(ROW_TILE, to_batch_major=False)
    return pl.pallas_call(
        functools.partial(_to_stream_kernel, n_ctx_tiles=n_ctx_tiles),
        grid=((cl + l) // steps,),
        in_specs=[
            pl.BlockSpec(pm.shape, lambda j: (0, 0)),
            pl.BlockSpec((b, steps, d), lambda j: (0, jnp.minimum(j, n_ctx_tiles - 1), 0)),
            pl.BlockSpec((b, steps, d), lambda j: (0, jnp.maximum(j - n_ctx_tiles, 0), 0)),
        ],
        out_specs=pl.BlockSpec((ROW_TILE, d), lambda j: (j, 0)),
        out_shape=jax.ShapeDtypeStruct(((cl + l) * b, d), F32),
        compiler_params=_cparams(("arbitrary",)),
        name="to_stream",
    )(pm, ctx, x)


def _adaln_kernel(c_ref, w_ref, b_ref, o_ref):
    s = jax.nn.silu(c_ref[...]).astype(BF16)
    o_ref[...] = jnp.dot(s, w_ref[...].astype(BF16), preferred_element_type=F32) + b_ref[...]


def _adaln(c2, ada_w, ada_b):
    depth, d, n = ada_w.shape
    tn = n // 4
    return pl.pallas_call(
        _adaln_kernel,
        grid=(depth, n // tn),
        in_specs=[
            pl.BlockSpec(c2.shape, lambda i, j: (0, 0)),
            pl.BlockSpec((None, d, tn), lambda i, j: (i, 0, j)),
            pl.BlockSpec((None, 1, tn), lambda i, j: (i, 0, j)),
        ],
        out_specs=pl.BlockSpec((None, c2.shape[0], tn), lambda i, j: (i, 0, j)),
        out_shape=jax.ShapeDtypeStruct((depth, c2.shape[0], n), F32),
        compiler_params=_cparams(("arbitrary", "arbitrary")),
        name="adaln",
    )(c2, ada_w, ada_b.reshape(depth, 1, n))


def _s5_scan_kernel(x_ref, mod_ref, g_ref, d_ref, a_ref, bw_ref, cw_ref, y_ref, bu_ref, st_ref,
                    *, steps, reverse):
    d_model = x_ref.shape[1]
    n_in, st_in, ch_in = bw_ref.shape
    n_out, st_out, ch_out = cw_ref.shape
    half = st_out // 2

    @pl.when(pl.program_id(0) == 0)
    def _():
        st_ref[...] = jnp.zeros_like(st_ref)

    mod = mod_ref[...]
    u = _norm_mod(x_ref[...], g_ref[...], mod[:, :d_model], mod[:, d_model:])
    ub = u.astype(BF16)

    for j in range(n_in):
        bu_ref[:, j * st_in:(j + 1) * st_in] = lax.dot_general(
            ub[:, j * ch_in:(j + 1) * ch_in], bw_ref[j], (((1,), (1,)), ((), ())),
            preferred_element_type=F32)

    order = range(steps - 1, -1, -1) if reverse else range(steps)
    for c in range(n_out):
        re = slice(c * st_out, c * st_out + half)
        im = slice(c * st_out + half, (c + 1) * st_out)
        sc = slice(c * half, (c + 1) * half)
        ar = a_ref[0, :, sc]
        ai = a_ref[1, :, sc]
        hr = st_ref[0, :, sc]
        hi = st_ref[1, :, sc]
        for t in order:
            rows = slice(t * SUBLANES, (t + 1) * SUBLANES)
            nr = ar * hr - ai * hi + bu_ref[rows, re]
            ni = ar * hi + ai * hr + bu_ref[rows, im]
            bu_ref[rows, re] = nr
            bu_ref[rows, im] = ni
            hr, hi = nr, ni
        st_ref[0, :, sc] = hr
        st_ref[1, :, sc] = hi

        hb = bu_ref[:, c * st_out:(c + 1) * st_out].astype(BF16)
        y = jnp.dot(hb, cw_ref[c], preferred_element_type=F32)
        cs = slice(c * ch_out, (c + 1) * ch_out)
        if not reverse:
            y = y + d_ref[:, cs] * u[:, cs]
        y_ref[:, cs] = y


def _s5_scan(xs, mod, layer, g, dvec, abar, bw, cw, n_ctx_steps, reverse):
    rows, d = xs.shape
    tile = SCAN_STEPS * SUBLANES
    n_tiles = rows // tile
    n_ctx = n_ctx_steps // SCAN_STEPS
    n_states = abar.shape[-1]

    def tile_of(k):
        if not reverse:
            return k
        return jnp.where(k < n_ctx, n_ctx - 1 - k, n_tiles - 1 + n_ctx - k)

    def is_ctx(k):
        return jnp.where(tile_of(k) < n_ctx, 1, 0)

    return pl.pallas_call(
        functools.partial(_s5_scan_kernel, steps=SCAN_STEPS, reverse=reverse),
        grid=(n_tiles,),
        in_specs=[
            pl.BlockSpec((tile, d), lambda k: (tile_of(k), 0)),
            pl.BlockSpec((None, None, SUBLANES, 2 * d), lambda k: (layer, is_ctx(k), 0, 0)),
            pl.BlockSpec((1, d), lambda k: (0, 0)),
            pl.BlockSpec((1, d), lambda k: (0, 0)),
            pl.BlockSpec(abar.shape, lambda k: (0, 0, 0)),
            pl.BlockSpec(bw.shape, lambda k: (0, 0, 0)),
            pl.BlockSpec(cw.shape, lambda k: (0, 0, 0)),
        ],
        out_specs=pl.BlockSpec((tile, d), lambda k: (tile_of(k), 0)),
        out_shape=jax.ShapeDtypeStruct((rows, d), F32),
        scratch_shapes=[
            pltpu.VMEM((tile, 2 * n_states), F32),
            pltpu.VMEM((2, SUBLANES, n_states), F32),
        ],
        compiler_params=_cparams(("arbitrary",)),
        name="s5_scan_bwd" if reverse else "s5_scan_fwd",
    )(xs, mod, g, dvec, abar, bw, cw)


def _glu_kernel(yf_ref, yb_ref, x_ref, g1_ref, w_ref, o_ref):
    d_model = x_ref.shape[1]
    y = yf_ref[...] + yb_ref[...]
    z = jnp.dot(jax.nn.gelu(y).astype(BF16), w_ref[...], preferred_element_type=F32)
    glu = z[:, :d_model] * jax.nn.sigmoid(z[:, d_model:])
    o_ref[...] = x_ref[...] + _per_batch(g1_ref[...], glu.shape[0]) * glu


def _glu_out(yf, yb, xs, mod, layer, w_glu, n_ctx_tiles):
    rows, d = xs.shape

    def is_ctx(i):
        return jnp.where(i < n_ctx_tiles, 1, 0)

    return pl.pallas_call(
        _glu_kernel,
        grid=(rows // ROW_TILE,),
        in_specs=[
            pl.BlockSpec((ROW_TILE, d), lambda i: (i, 0)),
            pl.BlockSpec((ROW_TILE, d), lambda i: (i, 0)),
            pl.BlockSpec((ROW_TILE, d), lambda i: (i, 0)),
            pl.BlockSpec((None, None, SUBLANES, d), lambda i: (layer, is_ctx(i), 0, 2)),
            pl.BlockSpec(w_glu.shape, lambda i: (0, 0)),
        ],
        out_specs=pl.BlockSpec((ROW_TILE, d), lambda i: (i, 0)),
        out_shape=jax.ShapeDtypeStruct((rows, d), F32),
        compiler_params=_cparams(("arbitrary",)),
        name="s5_glu_out",
    )(yf, yb, xs, mod, w_glu)


def _ffn_kernel(x_ref, mod_ref, g_ref, w1_ref, w2_ref, o_ref, *, ff_chunk):
    d_model = x_ref.shape[1]
    x = x_ref[...]
    mod = mod_ref[...]
    h = _norm_mod(x, g_ref[...], mod[:, :d_model], mod[:, d_model:2 * d_model]).astype(BF16)
    acc = jnp.zeros(x.shape, F32)
    for c in range(w1_ref.shape[1] // ff_chunk):
        cs = slice(c * ff_chunk, (c + 1) * ff_chunk)
        a = jnp.dot(h, w1_ref[:, cs], preferred_element_type=F32)
        a = jnp.square(jnp.maximum(a, 0.0)).astype(BF16)
        acc = acc + jnp.dot(a, w2_ref[cs, :], preferred_element_type=F32)
    o_ref[...] = x + _per_batch(mod[:, 2 * d_model:], x.shape[0]) * acc


def _ffn(xs, mod, layer, g, w1, w2, n_ctx_tiles):
    rows, d = xs.shape

    def is_ctx(i):
        return jnp.where(i < n_ctx_tiles, 1, 0)

    return pl.pallas_call(
        functools.partial(_ffn_kernel, ff_chunk=d),
        grid=(rows // ROW_TILE,),
        in_specs=[
            pl.BlockSpec((ROW_TILE, d), lambda i: (i, 0)),
            pl.BlockSpec((None, None, SUBLANES, 3 * d), lambda i: (layer, is_ctx(i), 0, 1)),
            pl.BlockSpec((1, d), lambda i: (0, 0)),
            pl.BlockSpec(w1.shape, lambda i: (0, 0)),
            pl.BlockSpec(w2.shape, lambda i: (0, 0)),
        ],
        out_specs=pl.BlockSpec((ROW_TILE, d), lambda i: (i, 0)),
        out_shape=jax.ShapeDtypeStruct((rows, d), F32),
        compiler_params=_cparams(("arbitrary",)),
        name="ffn",
    )(xs, mod, g, w1, w2)


def _qkv_kernel(x_ref, mod_ref, g_ref, pm_ref, w_ref, gain_ref, cos_ref, sin_ref,
                q_ref, k_ref, v_ref, *, head_dim):
    d_model = x_ref.shape[1]
    n_batch, steps, nq = q_ref.shape
    nk = k_ref.shape[2]
    mod = mod_ref[...]
    h = _norm_mod(x_ref[...], g_ref[...], mod[:, :d_model], mod[:, d_model:]).astype(BF16)
    h = jnp.dot(pm_ref[...], h, preferred_element_type=F32).astype(BF16)
    qkv = jnp.dot(h, w_ref[...], preferred_element_type=F32)
    rows = qkv.shape[0]

    lane = lax.broadcasted_iota(jnp.int32, (rows, LANES), 1)
    low_head = lane < head_dim
    first_half = (lane & (head_dim // 4)) == 0
    cos = jnp.tile(cos_ref[...], (n_batch, 1))
    sin = jnp.tile(sin_ref[...], (n_batch, 1))

    def put(ref, cs, val):
        for b in range(n_batch):
            ref[b, :, cs] = val[b * steps:(b + 1) * steps]

    def norm_rope(blk, gain):
        sq = blk * blk
        lo = jnp.sum(jnp.where(low_head, sq, 0.0), axis=-1, keepdims=True)
        hi = jnp.sum(jnp.where(low_head, 0.0, sq), axis=-1, keepdims=True)
        r = lax.rsqrt(jnp.where(low_head, lo, hi) * (1.0 / head_dim) + NORM_EPS)
        xn = (blk * r) * gain
        pair = jnp.where(first_half,
                         pltpu.roll(xn, LANES - head_dim // 4, axis=1),
                         pltpu.roll(xn, head_dim // 4, axis=1))
        return xn * cos + pair * sin

    for j in range(nq // LANES):
        cs = slice(j * LANES, (j + 1) * LANES)
        put(q_ref, cs, norm_rope(qkv[:, cs], gain_ref[0:1, :]).astype(BF16))
    for j in range(nk // LANES):
        cs = slice(j * LANES, (j + 1) * LANES)
        put(k_ref, cs, norm_rope(qkv[:, nq + j * LANES:nq + (j + 1) * LANES],
                                 gain_ref[1:2, :]).astype(BF16))
    for j in range(nk // LANES):
        vv = qkv[:, nq + nk + j * LANES:nq + nk + (j + 1) * LANES]
        put(v_ref, slice(2 * j * LANES, (2 * j + 1) * LANES),
            jnp.where(low_head, vv, 1.0).astype(BF16))
        put(v_ref, slice((2 * j + 1) * LANES, (2 * j + 2) * LANES),
            jnp.where(low_head, 1.0, vv).astype(BF16))


def _qkv(xs, mod, layer, g, w, gains, cos, sin, n_ctx_tiles, head_dim):
    rows, d = xs.shape
    nq = N_HEADS * head_dim
    nk = N_KV_HEADS * LANES
    steps = ROW_TILE // SUBLANES
    t_all = rows // SUBLANES
    pm = _row_perm(ROW_TILE, to_batch_major=True)

    def is_ctx(i):
        return jnp.where(i < n_ctx_tiles, 1, 0)

    return pl.pallas_call(
        functools.partial(_qkv_kernel, head_dim=head_dim),
        grid=(rows // ROW_TILE,),
        in_specs=[
            pl.BlockSpec((ROW_TILE, d), lambda i: (i, 0)),
            pl.BlockSpec((None, None, SUBLANES, 2 * d), lambda i: (layer, is_ctx(i), 0, 0)),
            pl.BlockSpec((1, d), lambda i: (0, 0)),
            pl.BlockSpec(pm.shape, lambda i: (0, 0)),
            pl.BlockSpec(w.shape, lambda i: (0, 0)),
            pl.BlockSpec(gains.shape, lambda i: (0, 0)),
            pl.BlockSpec((steps, LANES), lambda i: (i, 0)),
            pl.BlockSpec((steps, LANES), lambda i: (i, 0)),
        ],
        out_specs=[
            pl.BlockSpec((SUBLANES, steps, nq), lambda i: (0, i, 0)),
            pl.BlockSpec((SUBLANES, steps, nk), lambda i: (0, i, 0)),
            pl.BlockSpec((SUBLANES, steps, 2 * nk), lambda i: (0, i, 0)),
        ],
        out_shape=[
            jax.ShapeDtypeStruct((SUBLANES, t_all, nq), BF16),
            jax.ShapeDtypeStruct((SUBLANES, t_all, nk), BF16),
            jax.ShapeDtypeStruct((SUBLANES, t_all, 2 * nk), BF16),
        ],
        compiler_params=_cparams(("arbitrary",)),
        name="attn_qkv",
    )(xs, mod, g, pm, w, gains, cos, sin)


def _attn_kernel(q_ref, k_ref, v_ref, o_ref, *, head_dim, n_ctx, first_tile):
    tq = q_ref.shape[0]
    lane = lax.broadcasted_iota(jnp.int32, (tq, LANES), 1)
    low_head = lane < head_dim

    def attend(n_keys):
        k = k_ref[0:n_keys, :]
        for j in range(q_ref.shape[1] // LANES):
            qblk = q_ref[:, j * LANES:(j + 1) * LANES].astype(F32)
            res = []
            for hi_head in (0, 1):
                keep = jnp.logical_not(low_head) if hi_head else low_head
                qh = jnp.where(keep, qblk, 0.0).astype(BF16)
                s = lax.dot_general(qh, k, (((1,), (1,)), ((), ())), preferred_element_type=F32)
                e = jnp.exp2(s - jnp.max(s, axis=-1, keepdims=True)).astype(BF16)
                v = v_ref[0:n_keys, hi_head * LANES:(hi_head + 1) * LANES]
                res.append(jnp.dot(e, v, preferred_element_type=F32))
            num = jnp.where(low_head, res[0], res[1])
            den = pltpu.roll(jnp.where(low_head, res[1], res[0]), head_dim, axis=1)
            o_ref[:, j * LANES:(j + 1) * LANES] = (num / den).astype(BF16)

    if first_tile == 0:
        @pl.when(pl.program_id(2) == 0)
        def _():
            attend(n_ctx)

        @pl.when(pl.program_id(2) > 0)
        def _():
            attend(k_ref.shape[0])
    else:
        attend(k_ref.shape[0])


def _attention(q, k2, v2, n_ctx, first_tile, head_dim):
    n_batch, t_all, nq = q.shape
    qw = Q_PER_KV * head_dim
    n_tiles = t_all // Q_TILE - first_tile
    return pl.pallas_call(
        functools.partial(_attn_kernel, head_dim=head_dim, n_ctx=n_ctx, first_tile=first_tile),
        grid=(n_batch, N_KV_HEADS, n_tiles),
        in_specs=[
            pl.BlockSpec((None, Q_TILE, qw), lambda b, h, i: (b, i + first_tile, h)),
            pl.BlockSpec((None, t_all, LANES), lambda b, h, i: (b, 0, h)),
            pl.BlockSpec((None, t_all, 2 * LANES), lambda b, h, i: (b, 0, h)),
        ],
        out_specs=pl.BlockSpec((None, Q_TILE, qw), lambda b, h, i: (b, i, h)),
        out_shape=jax.ShapeDtypeStruct((n_batch, n_tiles * Q_TILE, nq), BF16),
        compiler_params=_cparams(("arbitrary", "arbitrary", "arbitrary")),
        name="attention",
    )(q, k2, v2)


def _attn_out_kernel(o_ref, x_ref, g1_ref, pm_ref, w_ref, y_ref):
    rows = x_ref.shape[0]
    o = o_ref[...].reshape(rows, -1)
    o = jnp.dot(pm_ref[...], o, preferred_element_type=F32).astype(BF16)
    y = jnp.dot(o, w_ref[...], preferred_element_type=F32)
    y_ref[...] = x_ref[...] + _per_batch(g1_ref[...], rows) * y


def _attn_out(o, xs, mod, layer, w_o, first_row_tile, n_ctx_tiles):
    n_batch, n, d = o.shape
    steps = ROW_TILE // n_batch
    pm = _row_perm(ROW_TILE, to_batch_major=False)

    def is_ctx(i):
        return jnp.where(i + first_row_tile < n_ctx_tiles, 1, 0)

    return pl.pallas_call(
        _attn_out_kernel,
        grid=(n // steps,),
        in_specs=[
            pl.BlockSpec((n_batch, steps, d), lambda i: (0, i, 0)),
            pl.BlockSpec((ROW_TILE, d), lambda i: (i + first_row_tile, 0)),
            pl.BlockSpec((None, None, SUBLANES, d), lambda i: (layer, is_ctx(i), 0, 2)),
            pl.BlockSpec(pm.shape, lambda i: (0, 0)),
            pl.BlockSpec(w_o.shape, lambda i: (0, 0)),
        ],
        out_specs=pl.BlockSpec((ROW_TILE, d), lambda i: (i, 0)),
        out_shape=jax.ShapeDtypeStruct((n * n_batch, d), F32),
        compiler_params=_cparams(("arbitrary",)),
        name="attn_out",
    )(o, xs, mod, pm, w_o)


def _final_kernel(x_ref, g_ref, pm_ref, o_ref):
    x = x_ref[...]
    y = (x * _rms(x)) * g_ref[...]
    o_ref[...] = _permute_exact(pm_ref[...], y).reshape(o_ref.shape)


def _final_norm(xs, g, n_batch, seq, first_row_tile):
    d = xs.shape[1]
    steps = ROW_TILE // n_batch
    pm = _row_perm(ROW_TILE, to_batch_major=True)
    return pl.pallas_call(
        _final_kernel,
        grid=(seq // steps,),
        in_specs=[
            pl.BlockSpec((ROW_TILE, d), lambda j: (j + first_row_tile, 0)),
            pl.BlockSpec((1, d), lambda j: (0, 0)),
            pl.BlockSpec(pm.shape, lambda j: (0, 0)),
        ],
        out_specs=pl.BlockSpec((n_batch, steps, d), lambda j: (0, j, 0)),
        out_shape=jax.ShapeDtypeStruct((n_batch, seq, d), F32),
        compiler_params=_cparams(("arbitrary",)),
        name="final_norm",
    )(xs, g, pm)


def _block_diag(blocks, per_block):
    g, r, c = blocks.shape
    nb = g // per_block
    eye = jnp.eye(per_block, dtype=blocks.dtype)
    out = jnp.einsum('ngrc,gk->ngrkc', blocks.reshape(nb, per_block, r, c), eye)
    return out.reshape(nb, per_block * r, per_block * c)


def _s5_params(a_re, a_im, log_dt, b_re, b_im, c_re, c_im):
    per_block = S5_OUT_BLOCK // S5_GROUP
    out = []
    for dirn in range(2):
        ar = a_re[dirn].astype(F32)
        ai = a_im[dirn].astype(F32)
        dt = jnp.exp(log_dt[dirn].astype(F32))[:, None]
        mag = jnp.exp(dt * ar)
        abar_re = mag * jnp.cos(dt * ai)
        abar_im = mag * jnp.sin(dt * ai)
        den = ar * ar + ai * ai
        nr = abar_re - 1.0
        f_re = (nr * ar + abar_im * ai) / den
        f_im = (abar_im * ar - nr * ai) / den
        br = b_re[dirn].astype(F32)
        bi = b_im[dirn].astype(F32)
        bb_re = f_re[..., None] * br - f_im[..., None] * bi
        bb_im = f_re[..., None] * bi + f_im[..., None] * br
        bw = jnp.concatenate([_block_diag(jnp.swapaxes(bb_re, 1, 2), per_block),
                              _block_diag(jnp.swapaxes(bb_im, 1, 2), per_block)], axis=-1)
        bw = jnp.swapaxes(_block_diag(bw, S5_IN_BLOCK // S5_OUT_BLOCK), 1, 2)
        cw = jnp.concatenate([_block_diag(jnp.swapaxes(c_re[dirn].astype(F32), 1, 2), per_block),
                              _block_diag(-jnp.swapaxes(c_im[dirn].astype(F32), 1, 2), per_block)],
                             axis=1)
        ab = jnp.stack([abar_re.reshape(-1), abar_im.reshape(-1)])
        ab = jnp.broadcast_to(ab[:, None, :], (2, SUBLANES, ab.shape[-1]))
        out.append((ab, bw.astype(BF16), cw.astype(BF16)))
    return out


def _rope_tables(n_ctx, seq, head_dim):
    axis_dim = head_dim // 2
    n_freq = axis_dim // 2
    pos = jnp.arange(seq, dtype=jnp.int32)
    row = (pos // GRID_W).astype(F32)
    col = (pos % GRID_W).astype(F32)
    inv_freq = ROPE_THETA ** (-jnp.arange(n_freq, dtype=F32) / n_freq)
    ang_row = row[:, None] * inv_freq[None, :]
    ang_col = col[:, None] * inv_freq[None, :]
    cos = jnp.concatenate([jnp.cos(ang_row)] * 2 + [jnp.cos(ang_col)] * 2, axis=-1)
    sin = jnp.concatenate([-jnp.sin(ang_row), jnp.sin(ang_row),
                           -jnp.sin(ang_col), jnp.sin(ang_col)], axis=-1)
    cos = jnp.concatenate([jnp.ones((n_ctx, head_dim), F32), cos], axis=0)
    sin = jnp.concatenate([jnp.zeros((n_ctx, head_dim), F32), sin], axis=0)
    per_block = LANES // head_dim
    return jnp.tile(cos, (1, per_block)), jnp.tile(sin, (1, per_block))


def _qkv_params(w_qkv, q_g, k_g, head_dim):
    d = w_qkv.shape[0]
    nq = N_HEADS * head_dim
    nkv = N_KV_HEADS * head_dim
    per_block = LANES // head_dim

    def dup(w):
        w = w.reshape(d, N_KV_HEADS, 1, head_dim)
        return jnp.broadcast_to(w, (d, N_KV_HEADS, per_block, head_dim)).reshape(d, -1)

    w = jnp.concatenate([w_qkv[:, :nq], dup(w_qkv[:, nq:nq + nkv]), dup(w_qkv[:, nq + nkv:])], axis=1)
    q_scale = (head_dim ** -0.5) * LOG2_E
    gains = jnp.stack([jnp.tile(q_g * q_scale, per_block), jnp.tile(k_g, per_block)])
    return w.astype(BF16), gains


def kernel(x, c, ctx, c_ctx, ada_w, ada_b, norm_mix_g, norm_ffn_g, s5_a_re, s5_a_im, s5_log_dt,
           s5_b_re, s5_b_im, s5_c_re, s5_c_im, s5_d, s5_w_glu, attn_w_qkv, attn_q_g, attn_k_g,
           attn_w_o, ffn_w1, ffn_w2, final_g):
    n_batch, seq, d = x.shape
    n_ctx = ctx.shape[1]
    depth = ada_w.shape[0]
    head_dim = d // N_HEADS
    t_all = n_ctx + seq
    assert n_batch == SUBLANES and n_ctx == Q_TILE and seq % Q_TILE == 0
    assert n_ctx % SCAN_STEPS == 0 and seq % SCAN_STEPS == 0 and d % S5_IN_BLOCK == 0
    ctx_row_tiles = n_ctx * n_batch // ROW_TILE

    xs = _to_stream(x, ctx)
    c2 = jnp.concatenate([c, jnp.broadcast_to(c_ctx[None, :], (SUBLANES, d))], axis=0)
    mod = _adaln(c2, ada_w, ada_b).reshape(depth, 2, SUBLANES, 6 * d)
    cos, sin = _rope_tables(n_ctx, seq, head_dim)
    has_ctx = True

    for i in range(depth):
        last = i == depth - 1
        j = i // 2
        g_mix = norm_mix_g[i].reshape(1, d)
        g_ffn = norm_ffn_g[i].reshape(1, d)
        if i % 2 == 0:
            fwd, bwd = _s5_params(s5_a_re[j], s5_a_im[j], s5_log_dt[j], s5_b_re[j], s5_b_im[j],
                                  s5_c_re[j], s5_c_im[j])
            dvec = s5_d[j].reshape(1, d)
            yf = _s5_scan(xs, mod, i, g_mix, dvec, *fwd, n_ctx, reverse=False)
            yb = _s5_scan(xs, mod, i, g_mix, dvec, *bwd, n_ctx, reverse=True)
            xs = _glu_out(yf, yb, xs, mod, i, s5_w_glu[j].astype(BF16), ctx_row_tiles)
            n_ctx_tiles = ctx_row_tiles
        else:
            w, gains = _qkv_params(attn_w_qkv[j], attn_q_g[j], attn_k_g[j], head_dim)
            q, k2, v2 = _qkv(xs, mod, i, g_mix, w, gains, cos, sin, ctx_row_tiles, head_dim)
            first_tile = 1 if last else 0
            o = _attention(q, k2, v2, n_ctx, first_tile, head_dim)
            xs = _attn_out(o, xs, mod, i, attn_w_o[j].astype(BF16),
                           first_tile * ctx_row_tiles, ctx_row_tiles)
            has_ctx = not last
            n_ctx_tiles = ctx_row_tiles if has_ctx else 0
        xs = _ffn(xs, mod, i, g_ffn, ffn_w1[i].astype(BF16), ffn_w2[i].astype(BF16), n_ctx_tiles)

    return _final_norm(xs, final_g.reshape(1, d), n_batch, seq, ctx_row_tiles if has_ctx else 0)
```

```python
import functools

import numpy as np
import jax
import jax.numpy as jnp
from jax import lax
from jax.experimental import pallas as pl
from jax.experimental.pallas import tpu as pltpu

F32 = jnp.float32
BF16 = jnp.bfloat16

GRID_W = 64
N_HEADS = 16
N_KV_HEADS = 4
Q_PER_KV = N_HEADS // N_KV_HEADS
S5_GROUP = 16
ROPE_THETA = 10000.0
NORM_EPS = 1e-6
LOG2_E = 1.4426950408889634

SUBLANES = 8
LANES = 128
MXU_DIM = 256

ROW_TILE = 256
SCAN_STEPS = 32
Q_TILE = 256
SCORE_LOOKAHEAD = 2
S5_IN_BLOCK = LANES
S5_OUT_BLOCK = LANES
VMEM_LIMIT = 48 * 1024 * 1024


def _cparams(semantics):
    return pltpu.CompilerParams(dimension_semantics=semantics, vmem_limit_bytes=VMEM_LIMIT)


def _per_batch(v, rows):
    return jnp.tile(v, (rows // SUBLANES, 1))


def _rms(x):
    return lax.rsqrt(jnp.mean(x * x, axis=-1, keepdims=True) + NORM_EPS)


def _norm_mod(x, g, shift, scale):
    h = (x * _rms(x)) * g
    return h * _per_batch(1.0 + scale, x.shape[0]) + _per_batch(shift, x.shape[0])


def _row_perm(rows, to_batch_major):
    nt = rows // SUBLANES
    r = np.arange(rows)
    if to_batch_major:
        src = (r % nt) * SUBLANES + r // nt
    else:
        src = (r % SUBLANES) * nt + r // SUBLANES
    return jnp.asarray(np.equal(src[:, None], r[None, :]), dtype=BF16)


def _permute_exact(pm, x):
    hi = x.astype(BF16)
    r1 = x - hi.astype(F32)
    mid = r1.astype(BF16)
    lo = (r1 - mid.astype(F32)).astype(BF16)
    return (jnp.dot(pm, hi, preferred_element_type=F32)
            + jnp.dot(pm, mid, preferred_element_type=F32)
            + jnp.dot(pm, lo, preferred_element_type=F32))


def _to_stream_kernel(pm_ref, ctx_ref, x_ref, o_ref, *, n_ctx_tiles):
    j = pl.program_id(0)
    rows = o_ref.shape[0]

    @pl.when(j < n_ctx_tiles)
    def _():
        o_ref[...] = _permute_exact(pm_ref[...], ctx_ref[...].reshape(rows, -1))

    @pl.when(j >= n_ctx_tiles)
    def _():
        o_ref[...] = _permute_exact(pm_ref[...], x_ref[...].reshape(rows, -1))


def _to_stream(x, ctx):
    b, l, d = x.shape
    cl = ctx.shape[1]
    steps = ROW_TILE // b
    n_ctx_tiles = cl // steps
    pm = _row_perm(ROW_TILE, to_batch_major=False)
    return pl.pallas_call(
        functools.partial(_to_stream_kernel, n_ctx_tiles=n_ctx_tiles),
        grid=((cl + l) // steps,),
        in_specs=[
            pl.BlockSpec(pm.shape, lambda j: (0, 0)),
            pl.BlockSpec((b, steps, d), lambda j: (0, jnp.minimum(j, n_ctx_tiles - 1), 0)),
            pl.BlockSpec((b, steps, d), lambda j: (0, jnp.maximum(j - n_ctx_tiles, 0), 0)),
        ],
        out_specs=pl.BlockSpec((ROW_TILE, d), lambda j: (j, 0)),
        out_shape=jax.ShapeDtypeStruct(((cl + l) * b, d), F32),
        compiler_params=_cparams(("arbitrary",)),
        name="to_stream",
    )(pm, ctx, x)


def _adaln_kernel(c_ref, w_ref, b_ref, o_ref):
    s = jax.nn.silu(c_ref[...]).astype(BF16)
    o_ref[...] = jnp.dot(s, w_ref[...].astype(BF16), preferred_element_type=F32) + b_ref[...]


def _adaln(c2, ada_w, ada_b):
    depth, d, n = ada_w.shape
    tn = n // 4
    return pl.pallas_call(
        _adaln_kernel,
        grid=(depth, n // tn),
        in_specs=[
            pl.BlockSpec(c2.shape, lambda i, j: (0, 0)),
            pl.BlockSpec((None, d, tn), lambda i, j: (i, 0, j)),
            pl.BlockSpec((None, 1, tn), lambda i, j: (i, 0, j)),
        ],
        out_specs=pl.BlockSpec((None, c2.shape[0], tn), lambda i, j: (i, 0, j)),
        out_shape=jax.ShapeDtypeStruct((depth, c2.shape[0], n), F32),
        compiler_params=_cparams(("arbitrary", "arbitrary")),
        name="adaln",
    )(c2, ada_w, ada_b.reshape(depth, 1, n))


def _s5_scan_kernel(x_ref, mod_ref, g_ref, d_ref, a_ref, bw_ref, cw_ref, y_ref, bu_ref, st_ref,
                    *, steps, reverse):
    d_model = x_ref.shape[1]
    n_in, st_in, ch_in = bw_ref.shape
    n_out, st_out, ch_out = cw_ref.shape
    half = st_out // 2

    @pl.when(pl.program_id(0) == 0)
    def _():
        st_ref[...] = jnp.zeros_like(st_ref)

    mod = mod_ref[...]
    u = _norm_mod(x_ref[...], g_ref[...], mod[:, :d_model], mod[:, d_model:])
    ub = u.astype(BF16)

    for j in range(n_in):
        bu_ref[:, j * st_in:(j + 1) * st_in] = lax.dot_general(
            ub[:, j * ch_in:(j + 1) * ch_in], bw_ref[j], (((1,), (1,)), ((), ())),
            preferred_element_type=F32)

    order = range(steps - 1, -1, -1) if reverse else range(steps)
    for c in range(n_out):
        re = slice(c * st_out, c * st_out + half)
        im = slice(c * st_out + half, (c + 1) * st_out)
        sc = slice(c * half, (c + 1) * half)
        ar = a_ref[0, :, sc]
        ai = a_ref[1, :, sc]
        hr = st_ref[0, :, sc]
        hi = st_ref[1, :, sc]
        for t in order:
            rows = slice(t * SUBLANES, (t + 1) * SUBLANES)
            nr = ar * hr - ai * hi + bu_ref[rows, re]
            ni = ar * hi + ai * hr + bu_ref[rows, im]
            bu_ref[rows, re] = nr
            bu_ref[rows, im] = ni
            hr, hi = nr, ni
        st_ref[0, :, sc] = hr
        st_ref[1, :, sc] = hi

        hb = bu_ref[:, c * st_out:(c + 1) * st_out].astype(BF16)
        y = jnp.dot(hb, cw_ref[c], preferred_element_type=F32)
        cs = slice(c * ch_out, (c + 1) * ch_out)
        if not reverse:
            y = y + d_ref[:, cs] * u[:, cs]
        y_ref[:, cs] = y


def _s5_scan(xs, mod, layer, g, dvec, abar, bw, cw, n_ctx_steps, reverse):
    rows, d = xs.shape
    tile = SCAN_STEPS * SUBLANES
    n_tiles = rows // tile
    n_ctx = n_ctx_steps // SCAN_STEPS
    n_states = abar.shape[-1]

    def tile_of(k):
        if not reverse:
            return k
        return jnp.where(k < n_ctx, n_ctx - 1 - k, n_tiles - 1 + n_ctx - k)

    def is_ctx(k):
        return jnp.where(tile_of(k) < n_ctx, 1, 0)

    return pl.pallas_call(
        functools.partial(_s5_scan_kernel, steps=SCAN_STEPS, reverse=reverse),
        grid=(n_tiles,),
        in_specs=[
            pl.BlockSpec((tile, d), lambda k: (tile_of(k), 0)),
            pl.BlockSpec((None, None, SUBLANES, 2 * d), lambda k: (layer, is_ctx(k), 0, 0)),
            pl.BlockSpec((1, d), lambda k: (0, 0)),
            pl.BlockSpec((1, d), lambda k: (0, 0)),
            pl.BlockSpec(abar.shape, lambda k: (0, 0, 0)),
            pl.BlockSpec(bw.shape, lambda k: (0, 0, 0)),
            pl.BlockSpec(cw.shape, lambda k: (0, 0, 0)),
        ],
        out_specs=pl.BlockSpec((tile, d), lambda k: (tile_of(k), 0)),
        out_shape=jax.ShapeDtypeStruct((rows, d), F32),
        scratch_shapes=[
            pltpu.VMEM((tile, 2 * n_states), F32),
            pltpu.VMEM((2, SUBLANES, n_states), F32),
        ],
        compiler_params=_cparams(("arbitrary",)),
        name="s5_scan_bwd" if reverse else "s5_scan_fwd",
    )(xs, mod, g, dvec, abar, bw, cw)


def _mlp_sublayer(x, mod, g, w1_ref, w2_ref):
    d = x.shape[1]
    h = _norm_mod(x, g, mod[:, 3 * d:4 * d], mod[:, 4 * d:5 * d]).astype(BF16)
    acc = jnp.zeros(x.shape, F32)
    for c in range(w1_ref.shape[1] // d):
        cs = slice(c * d, (c + 1) * d)
        a = jnp.dot(h, w1_ref[:, cs], preferred_element_type=F32)
        a = jnp.square(jnp.maximum(a, 0.0)).astype(BF16)
        acc = acc + jnp.dot(a, w2_ref[cs, :], preferred_element_type=F32)
    return x + _per_batch(mod[:, 5 * d:], x.shape[0]) * acc


def _s5_tail_kernel(yf_ref, yb_ref, x_ref, mod_ref, g_ref, wg_ref, w1_ref, w2_ref, o_ref):
    d = x_ref.shape[1]
    mod = mod_ref[...]
    y = yf_ref[...] + yb_ref[...]
    z = jnp.dot(jax.nn.gelu(y).astype(BF16), wg_ref[...], preferred_element_type=F32)
    glu = z[:, :d] * jax.nn.sigmoid(z[:, d:])
    x1 = x_ref[...] + _per_batch(mod[:, 2 * d:3 * d], glu.shape[0]) * glu
    o_ref[...] = _mlp_sublayer(x1, mod, g_ref[...], w1_ref, w2_ref)


def _attn_tail_kernel(o_ref, x_ref, mod_ref, g_ref, pm_ref, wo_ref, w1_ref, w2_ref, y_ref):
    rows, d = x_ref.shape
    mod = mod_ref[...]
    o = o_ref[...].reshape(rows, -1)
    o = jnp.dot(pm_ref[...], o, preferred_element_type=F32).astype(BF16)
    y = jnp.dot(o, wo_ref[...], preferred_element_type=F32)
    x1 = x_ref[...] + _per_batch(mod[:, 2 * d:3 * d], rows) * y
    y_ref[...] = _mlp_sublayer(x1, mod, g_ref[...], w1_ref, w2_ref)


def _resident(arr):
    return pl.BlockSpec(arr.shape, lambda i: (0,) * arr.ndim, pipeline_mode=pl.Buffered(1))


def _s5_tail(yf, yb, xs, mod, layer, g, w_glu, w1, w2, n_ctx_tiles):
    rows, d = xs.shape
    row_spec = pl.BlockSpec((ROW_TILE, d), lambda i: (i, 0))
    return pl.pallas_call(
        _s5_tail_kernel,
        grid=(rows // ROW_TILE,),
        in_specs=[
            row_spec, row_spec, row_spec,
            pl.BlockSpec((None, None, SUBLANES, mod.shape[-1]),
                         lambda i: (layer, jnp.where(i < n_ctx_tiles, 1, 0), 0, 0)),
            _resident(g), _resident(w_glu), _resident(w1), _resident(w2),
        ],
        out_specs=row_spec,
        out_shape=jax.ShapeDtypeStruct((rows, d), F32),
        compiler_params=_cparams(("arbitrary",)),
        name="s5_tail",
    )(yf, yb, xs, mod, g, w_glu, w1, w2)


def _qkv_kernel(x_ref, mod_ref, g_ref, pm_ref, w_ref, gain_ref, cos_ref, sin_ref,
                q_ref, k_ref, v_ref, *, head_dim):
    d_model = x_ref.shape[1]
    n_batch, steps, nq = q_ref.shape
    nk = k_ref.shape[2]
    mod = mod_ref[...]
    h = _norm_mod(x_ref[...], g_ref[...], mod[:, :d_model], mod[:, d_model:]).astype(BF16)
    h = jnp.dot(pm_ref[...], h, preferred_element_type=F32).astype(BF16)
    qkv = jnp.dot(h, w_ref[...], preferred_element_type=F32)
    rows = qkv.shape[0]

    lane = lax.broadcasted_iota(jnp.int32, (rows, LANES), 1)
    low_head = lane < head_dim
    first_half = (lane & (head_dim // 4)) == 0
    cos = jnp.tile(cos_ref[...], (n_batch, 1))
    sin = jnp.tile(sin_ref[...], (n_batch, 1))

    def put(ref, cs, val):
        for b in range(n_batch):
            ref[b, :, cs] = val[b * steps:(b + 1) * steps]

    def norm_rope(blk, gain):
        sq = blk * blk
        lo = jnp.sum(jnp.where(low_head, sq, 0.0), axis=-1, keepdims=True)
        hi = jnp.sum(jnp.where(low_head, 0.0, sq), axis=-1, keepdims=True)
        r = lax.rsqrt(jnp.where(low_head, lo, hi) * (1.0 / head_dim) + NORM_EPS)
        xn = (blk * r) * gain
        pair = jnp.where(first_half,
                         pltpu.roll(xn, LANES - head_dim // 4, axis=1),
                         pltpu.roll(xn, head_dim // 4, axis=1))
        return xn * cos + pair * sin

    for j in range(nq // LANES):
        cs = slice(j * LANES, (j + 1) * LANES)
        put(q_ref, cs, norm_rope(qkv[:, cs], gain_ref[0:1, :]).astype(BF16))
    for j in range(nk // LANES):
        cs = slice(j * LANES, (j + 1) * LANES)
        put(k_ref, cs, norm_rope(qkv[:, nq + j * LANES:nq + (j + 1) * LANES],
                                 gain_ref[1:2, :]).astype(BF16))
    for j in range(nk // LANES):
        vv = qkv[:, nq + nk + j * LANES:nq + nk + (j + 1) * LANES]
        put(v_ref, slice(2 * j * LANES, (2 * j + 1) * LANES),
            jnp.where(low_head, vv, 1.0).astype(BF16))
        put(v_ref, slice((2 * j + 1) * LANES, (2 * j + 2) * LANES),
            jnp.where(low_head, 1.0, vv).astype(BF16))


def _qkv(xs, mod, layer, g, w, gains, cos, sin, n_ctx_tiles, head_dim):
    rows, d = xs.shape
    nq = N_HEADS * head_dim
    nk = N_KV_HEADS * LANES
    steps = ROW_TILE // SUBLANES
    t_all = rows // SUBLANES
    pm = _row_perm(ROW_TILE, to_batch_major=True)

    def is_ctx(i):
        return jnp.where(i < n_ctx_tiles, 1, 0)

    return pl.pallas_call(
        functools.partial(_qkv_kernel, head_dim=head_dim),
        grid=(rows // ROW_TILE,),
        in_specs=[
            pl.BlockSpec((ROW_TILE, d), lambda i: (i, 0)),
            pl.BlockSpec((None, None, SUBLANES, 2 * d), lambda i: (layer, is_ctx(i), 0, 0)),
            pl.BlockSpec((1, d), lambda i: (0, 0)),
            pl.BlockSpec(pm.shape, lambda i: (0, 0)),
            pl.BlockSpec(w.shape, lambda i: (0, 0)),
            pl.BlockSpec(gains.shape, lambda i: (0, 0)),
            pl.BlockSpec((steps, LANES), lambda i: (i, 0)),
            pl.BlockSpec((steps, LANES), lambda i: (i, 0)),
        ],
        out_specs=[
            pl.BlockSpec((SUBLANES, steps, nq), lambda i: (0, i, 0)),
            pl.BlockSpec((SUBLANES, steps, nk), lambda i: (0, i, 0)),
            pl.BlockSpec((SUBLANES, steps, 2 * nk), lambda i: (0, i, 0)),
        ],
        out_shape=[
            jax.ShapeDtypeStruct((SUBLANES, t_all, nq), BF16),
            jax.ShapeDtypeStruct((SUBLANES, t_all, nk), BF16),
            jax.ShapeDtypeStruct((SUBLANES, t_all, 2 * nk), BF16),
        ],
        compiler_params=_cparams(("arbitrary",)),
        name="attn_qkv",
    )(xs, mod, g, pm, w, gains, cos, sin)


def _attn_kernel(q_ref, k_ref, v_ref, o_ref, *, head_dim, n_ctx, first_tile):
    tq = q_ref.shape[0]
    lane = lax.broadcasted_iota(jnp.int32, (tq, LANES), 1)
    low_head = lane < head_dim

    def attend(n_keys):
        k = k_ref[0:n_keys, :]
        n_blk = q_ref.shape[1] // LANES

        def scores(i):
            j, hi_head = divmod(i, 2)
            qblk = q_ref[:, j * LANES:(j + 1) * LANES].astype(F32)
            keep = jnp.logical_not(low_head) if hi_head else low_head
            qh = jnp.where(keep, qblk, 0.0).astype(BF16)
            return lax.dot_general(qh, k, (((1,), (1,)), ((), ())), preferred_element_type=F32)

        res = []
        ahead = [scores(i) for i in range(min(SCORE_LOOKAHEAD, 2 * n_blk))]
        for i in range(2 * n_blk):
            s = ahead.pop(0)
            if i + SCORE_LOOKAHEAD < 2 * n_blk:
                ahead.append(scores(i + SCORE_LOOKAHEAD))
            e = jnp.exp2(s - jnp.max(s, axis=-1, keepdims=True)).astype(BF16)
            v = v_ref[0:n_keys, (i % 2) * LANES:(i % 2 + 1) * LANES]
            res.append(jnp.dot(e, v, preferred_element_type=F32))
        for j in range(n_blk):
            lo, hi = res[2 * j], res[2 * j + 1]
            num = jnp.where(low_head, lo, hi)
            den = pltpu.roll(jnp.where(low_head, hi, lo), head_dim, axis=1)
            o_ref[:, j * LANES:(j + 1) * LANES] = (num / den).astype(BF16)

    if first_tile == 0:
        @pl.when(pl.program_id(2) == 0)
        def _():
            attend(n_ctx)

        @pl.when(pl.program_id(2) > 0)
        def _():
            attend(k_ref.shape[0])
    else:
        attend(k_ref.shape[0])


def _attention(q, k2, v2, n_ctx, first_tile, head_dim):
    n_batch, t_all, nq = q.shape
    qw = Q_PER_KV * head_dim
    n_tiles = t_all // Q_TILE - first_tile
    return pl.pallas_call(
        functools.partial(_attn_kernel, head_dim=head_dim, n_ctx=n_ctx, first_tile=first_tile),
        grid=(n_batch, N_KV_HEADS, n_tiles),
        in_specs=[
            pl.BlockSpec((None, Q_TILE, qw), lambda b, h, i: (b, i + first_tile, h)),
            pl.BlockSpec((None, t_all, LANES), lambda b, h, i: (b, 0, h)),
            pl.BlockSpec((None, t_all, 2 * LANES), lambda b, h, i: (b, 0, h)),
        ],
        out_specs=pl.BlockSpec((None, Q_TILE, qw), lambda b, h, i: (b, i, h)),
        out_shape=jax.ShapeDtypeStruct((n_batch, n_tiles * Q_TILE, nq), BF16),
        compiler_params=_cparams(("arbitrary", "arbitrary", "arbitrary")),
        name="attention",
    )(q, k2, v2)


def _attn_tail(o, xs, mod, layer, g, w_o, w1, w2, first_row_tile, n_ctx_tiles):
    n_batch, n, d = o.shape
    steps = ROW_TILE // n_batch
    pm = _row_perm(ROW_TILE, to_batch_major=False)

    def is_ctx(i):
        return jnp.where(i + first_row_tile < n_ctx_tiles, 1, 0)

    return pl.pallas_call(
        _attn_tail_kernel,
        grid=(n // steps,),
        in_specs=[
            pl.BlockSpec((n_batch, steps, d), lambda i: (0, i, 0)),
            pl.BlockSpec((ROW_TILE, d), lambda i: (i + first_row_tile, 0)),
            pl.BlockSpec((None, None, SUBLANES, mod.shape[-1]), lambda i: (layer, is_ctx(i), 0, 0)),
            _resident(g), _resident(pm), _resident(w_o), _resident(w1), _resident(w2),
        ],
        out_specs=pl.BlockSpec((ROW_TILE, d), lambda i: (i, 0)),
        out_shape=jax.ShapeDtypeStruct((n * n_batch, d), F32),
        compiler_params=_cparams(("arbitrary",)),
        name="attn_tail",
    )(o, xs, mod, g, pm, w_o, w1, w2)


def _final_kernel(x_ref, g_ref, pm_ref, o_ref):
    x = x_ref[...]
    y = (x * _rms(x)) * g_ref[...]
    o_ref[...] = _permute_exact(pm_ref[...], y).reshape(o_ref.shape)


def _final_norm(xs, g, n_batch, seq, first_row_tile):
    d = xs.shape[1]
    steps = ROW_TILE // n_batch
    pm = _row_perm(ROW_TILE, to_batch_major=True)
    return pl.pallas_call(
        _final_kernel,
        grid=(seq // steps,),
        in_specs=[
            pl.BlockSpec((ROW_TILE, d), lambda j: (j + first_row_tile, 0)),
            pl.BlockSpec((1, d), lambda j: (0, 0)),
            pl.BlockSpec(pm.shape, lambda j: (0, 0)),
        ],
        out_specs=pl.BlockSpec((n_batch, steps, d), lambda j: (0, j, 0)),
        out_shape=jax.ShapeDtypeStruct((n_batch, seq, d), F32),
        compiler_params=_cparams(("arbitrary",)),
        name="final_norm",
    )(xs, g, pm)


def _block_diag(blocks, per_block):
    g, r, c = blocks.shape
    nb = g // per_block
    eye = jnp.eye(per_block, dtype=blocks.dtype)
    out = jnp.einsum('ngrc,gk->ngrkc', blocks.reshape(nb, per_block, r, c), eye)
    return out.reshape(nb, per_block * r, per_block * c)


def _s5_params(a_re, a_im, log_dt, b_re, b_im, c_re, c_im):
    per_block = S5_OUT_BLOCK // S5_GROUP
    out = []
    for dirn in range(2):
        ar = a_re[dirn].astype(F32)
        ai = a_im[dirn].astype(F32)
        dt = jnp.exp(log_dt[dirn].astype(F32))[:, None]
        mag = jnp.exp(dt * ar)
        abar_re = mag * jnp.cos(dt * ai)
        abar_im = mag * jnp.sin(dt * ai)
        den = ar * ar + ai * ai
        nr = abar_re - 1.0
        f_re = (nr * ar + abar_im * ai) / den
        f_im = (abar_im * ar - nr * ai) / den
        br = b_re[dirn].astype(F32)
        bi = b_im[dirn].astype(F32)
        bb_re = f_re[..., None] * br - f_im[..., None] * bi
        bb_im = f_re[..., None] * bi + f_im[..., None] * br
        bw = jnp.concatenate([_block_diag(jnp.swapaxes(bb_re, 1, 2), per_block),
                              _block_diag(jnp.swapaxes(bb_im, 1, 2), per_block)], axis=-1)
        bw = jnp.swapaxes(_block_diag(bw, S5_IN_BLOCK // S5_OUT_BLOCK), 1, 2)
        cw = jnp.concatenate([_block_diag(jnp.swapaxes(c_re[dirn].astype(F32), 1, 2), per_block),
                              _block_diag(-jnp.swapaxes(c_im[dirn].astype(F32), 1, 2), per_block)],
                             axis=1)
        ab = jnp.stack([abar_re.reshape(-1), abar_im.reshape(-1)])
        ab = jnp.broadcast_to(ab[:, None, :], (2, SUBLANES, ab.shape[-1]))
        out.append((ab, bw.astype(BF16), cw.astype(BF16)))
    return out


def _rope_tables(n_ctx, seq, head_dim):
    axis_dim = head_dim // 2
    n_freq = axis_dim // 2
    pos = jnp.arange(seq, dtype=jnp.int32)
    row = (pos // GRID_W).astype(F32)
    col = (pos % GRID_W).astype(F32)
    inv_freq = ROPE_THETA ** (-jnp.arange(n_freq, dtype=F32) / n_freq)
    ang_row = row[:, None] * inv_freq[None, :]
    ang_col = col[:, None] * inv_freq[None, :]
    cos = jnp.concatenate([jnp.cos(ang_row)] * 2 + [jnp.cos(ang_col)] * 2, axis=-1)
    sin = jnp.concatenate([-jnp.sin(ang_row), jnp.sin(ang_row),
                           -jnp.sin(ang_col), jnp.sin(ang_col)], axis=-1)
    cos = jnp.concatenate([jnp.ones((n_ctx, head_dim), F32), cos], axis=0)
    sin = jnp.concatenate([jnp.zeros((n_ctx, head_dim), F32), sin], axis=0)
    per_block = LANES // head_dim
    return jnp.tile(cos, (1, per_block)), jnp.tile(sin, (1, per_block))


def _qkv_params(w_qkv, q_g, k_g, head_dim):
    d = w_qkv.shape[0]
    nq = N_HEADS * head_dim
    nkv = N_KV_HEADS * head_dim
    per_block = LANES // head_dim

    def dup(w):
        w = w.reshape(d, N_KV_HEADS, 1, head_dim)
        return jnp.broadcast_to(w, (d, N_KV_HEADS, per_block, head_dim)).reshape(d, -1)

    w = jnp.concatenate([w_qkv[:, :nq], dup(w_qkv[:, nq:nq + nkv]), dup(w_qkv[:, nq + nkv:])], axis=1)
    q_scale = (head_dim ** -0.5) * LOG2_E
    gains = jnp.stack([jnp.tile(q_g * q_scale, per_block), jnp.tile(k_g, per_block)])
    return w.astype(BF16), gains


def kernel(x, c, ctx, c_ctx, ada_w, ada_b, norm_mix_g, norm_ffn_g, s5_a_re, s5_a_im, s5_log_dt,
           s5_b_re, s5_b_im, s5_c_re, s5_c_im, s5_d, s5_w_glu, attn_w_qkv, attn_q_g, attn_k_g,
           attn_w_o, ffn_w1, ffn_w2, final_g):
    n_batch, seq, d = x.shape
    n_ctx = ctx.shape[1]
    depth = ada_w.shape[0]
    head_dim = d // N_HEADS
    t_all = n_ctx + seq
    assert n_batch == SUBLANES and n_ctx == Q_TILE and seq % Q_TILE == 0
    assert n_ctx % SCAN_STEPS == 0 and seq % SCAN_STEPS == 0 and d % S5_IN_BLOCK == 0
    ctx_row_tiles = n_ctx * n_batch // ROW_TILE

    xs = _to_stream(x, ctx)
    c2 = jnp.concatenate([c, jnp.broadcast_to(c_ctx[None, :], (SUBLANES, d))], axis=0)
    mod = _adaln(c2, ada_w, ada_b).reshape(depth, 2, SUBLANES, 6 * d)
    cos, sin = _rope_tables(n_ctx, seq, head_dim)
    has_ctx = True

    for i in range(depth):
        last = i == depth - 1
        j = i // 2
        g_mix = norm_mix_g[i].reshape(1, d)
        g_ffn = norm_ffn_g[i].reshape(1, d)
        w1 = ffn_w1[i].astype(BF16)
        w2 = ffn_w2[i].astype(BF16)
        if i % 2 == 0:
            fwd, bwd = _s5_params(s5_a_re[j], s5_a_im[j], s5_log_dt[j], s5_b_re[j], s5_b_im[j],
                                  s5_c_re[j], s5_c_im[j])
            dvec = s5_d[j].reshape(1, d)
            yf = _s5_scan(xs, mod, i, g_mix, dvec, *fwd, n_ctx, reverse=False)
            yb = _s5_scan(xs, mod, i, g_mix, dvec, *bwd, n_ctx, reverse=True)
            xs = _s5_tail(yf, yb, xs, mod, i, g_ffn, s5_w_glu[j].astype(BF16), w1, w2, ctx_row_tiles)
        else:
            w, gains = _qkv_params(attn_w_qkv[j], attn_q_g[j], attn_k_g[j], head_dim)
            q, k2, v2 = _qkv(xs, mod, i, g_mix, w, gains, cos, sin, ctx_row_tiles, head_dim)
            first_tile = 1 if last else 0
            o = _attention(q, k2, v2, n_ctx, first_tile, head_dim)
            xs = _attn_tail(o, xs, mod, i, g_ffn, attn_w_o[j].astype(BF16), w1, w2,
                            first_tile * ctx_row_tiles, ctx_row_tiles)
            has_ctx = not last

    return _final_norm(xs, final_g.reshape(1, d), n_batch, seq, ctx_row_tiles if has_ctx else 0)
```

```python
import functools

import numpy as np
import jax
import jax.numpy as jnp
from jax import lax
from jax.experimental import pallas as pl
from jax.experimental.pallas import tpu as pltpu

F32 = jnp.float32
BF16 = jnp.bfloat16

GRID_W = 64
N_HEADS = 16
N_KV_HEADS = 4
Q_PER_KV = N_HEADS // N_KV_HEADS
S5_GROUP = 16
ROPE_THETA = 10000.0
NORM_EPS = 1e-6
LOG2_E = 1.4426950408889634

SUBLANES = 8
LANES = 128
MXU_DIM = 256

ROW_TILE = 256
SCAN_STEPS = 32
Q_TILE = 512
SCORE_LOOKAHEAD = 2
S5_IN_BLOCK = LANES
S5_OUT_BLOCK = LANES
VMEM_LIMIT = 48 * 1024 * 1024


def _cparams(semantics):
    return pltpu.CompilerParams(dimension_semantics=semantics, vmem_limit_bytes=VMEM_LIMIT)


def _per_batch(v, rows):
    return jnp.tile(v, (rows // SUBLANES, 1))


def _rms(x):
    return lax.rsqrt(jnp.mean(x * x, axis=-1, keepdims=True) + NORM_EPS)


def _norm_mod(x, g, shift, scale):
    h = (x * _rms(x)) * g
    return h * _per_batch(1.0 + scale, x.shape[0]) + _per_batch(shift, x.shape[0])


def _row_perm(rows, to_batch_major):
    nt = rows // SUBLANES
    r = np.arange(rows)
    if to_batch_major:
        src = (r % nt) * SUBLANES + r // nt
    else:
        src = (r % SUBLANES) * nt + r // SUBLANES
    return jnp.asarray(np.equal(src[:, None], r[None, :]), dtype=BF16)


def _permute_exact(pm, x):
    hi = x.astype(BF16)
    r1 = x - hi.astype(F32)
    mid = r1.astype(BF16)
    lo = (r1 - mid.astype(F32)).astype(BF16)
    return (jnp.dot(pm, hi, preferred_element_type=F32)
            + jnp.dot(pm, mid, preferred_element_type=F32)
            + jnp.dot(pm, lo, preferred_element_type=F32))


def _to_stream_kernel(pm_ref, x_ref, ctx_ref, o_ref, *, n_lat_tiles):
    j = pl.program_id(0)
    rows = o_ref.shape[0]

    @pl.when(j < n_lat_tiles)
    def _():
        o_ref[...] = _permute_exact(pm_ref[...], x_ref[...].reshape(rows, -1))

    @pl.when(j >= n_lat_tiles)
    def _():
        o_ref[...] = _permute_exact(pm_ref[...], ctx_ref[...].reshape(rows, -1))


def _to_stream(x, ctx):
    b, l, d = x.shape
    cl = ctx.shape[1]
    steps = ROW_TILE // b
    n_lat_tiles = l // steps
    pm = _row_perm(ROW_TILE, to_batch_major=False)
    return pl.pallas_call(
        functools.partial(_to_stream_kernel, n_lat_tiles=n_lat_tiles),
        grid=((cl + l) // steps,),
        in_specs=[
            pl.BlockSpec(pm.shape, lambda j: (0, 0)),
            pl.BlockSpec((b, steps, d), lambda j: (0, jnp.minimum(j, n_lat_tiles - 1), 0)),
            pl.BlockSpec((b, steps, d), lambda j: (0, jnp.maximum(j - n_lat_tiles, 0), 0)),
        ],
        out_specs=pl.BlockSpec((ROW_TILE, d), lambda j: (j, 0)),
        out_shape=jax.ShapeDtypeStruct(((cl + l) * b, d), F32),
        compiler_params=_cparams(("arbitrary",)),
        name="to_stream",
    )(pm, x, ctx)


def _adaln_kernel(c_ref, w_ref, b_ref, o_ref):
    s = jax.nn.silu(c_ref[...]).astype(BF16)
    o_ref[...] = jnp.dot(s, w_ref[...].astype(BF16), preferred_element_type=F32) + b_ref[...]


def _adaln(c2, ada_w, ada_b):
    depth, d, n = ada_w.shape
    tn = n // 4
    return pl.pallas_call(
        _adaln_kernel,
        grid=(depth, n // tn),
        in_specs=[
            pl.BlockSpec(c2.shape, lambda i, j: (0, 0)),
            pl.BlockSpec((None, d, tn), lambda i, j: (i, 0, j)),
            pl.BlockSpec((None, 1, tn), lambda i, j: (i, 0, j)),
        ],
        out_specs=pl.BlockSpec((None, c2.shape[0], tn), lambda i, j: (i, 0, j)),
        out_shape=jax.ShapeDtypeStruct((depth, c2.shape[0], n), F32),
        compiler_params=_cparams(("arbitrary", "arbitrary")),
        name="adaln",
    )(c2, ada_w, ada_b.reshape(depth, 1, n))


def _s5_scan_kernel(x_ref, mod_ref, g_ref, d_ref, a_ref, bw_ref, cw_ref, y_ref, bu_ref, st_ref,
                    *, steps, reverse):
    d_model = x_ref.shape[1]
    n_in, st_in, ch_in = bw_ref.shape
    n_out, st_out, ch_out = cw_ref.shape
    half = st_out // 2

    @pl.when(pl.program_id(0) == 0)
    def _():
        st_ref[...] = jnp.zeros_like(st_ref)

    mod = mod_ref[...]
    u = _norm_mod(x_ref[...], g_ref[...], mod[:, :d_model], mod[:, d_model:])
    ub = u.astype(BF16)

    for j in range(n_in):
        bu_ref[:, j * st_in:(j + 1) * st_in] = lax.dot_general(
            ub[:, j * ch_in:(j + 1) * ch_in], bw_ref[j], (((1,), (1,)), ((), ())),
            preferred_element_type=F32)

    order = range(steps - 1, -1, -1) if reverse else range(steps)
    for c in range(n_out):
        re = slice(c * st_out, c * st_out + half)
        im = slice(c * st_out + half, (c + 1) * st_out)
        sc = slice(c * half, (c + 1) * half)
        ar = a_ref[0, :, sc]
        ai = a_ref[1, :, sc]
        hr = st_ref[0, :, sc]
        hi = st_ref[1, :, sc]
        for t in order:
            rows = slice(t * SUBLANES, (t + 1) * SUBLANES)
            nr = ar * hr - ai * hi + bu_ref[rows, re]
            ni = ar * hi + ai * hr + bu_ref[rows, im]
            bu_ref[rows, re] = nr
            bu_ref[rows, im] = ni
            hr, hi = nr, ni
        st_ref[0, :, sc] = hr
        st_ref[1, :, sc] = hi

        hb = bu_ref[:, c * st_out:(c + 1) * st_out].astype(BF16)
        y = jnp.dot(hb, cw_ref[c], preferred_element_type=F32)
        cs = slice(c * ch_out, (c + 1) * ch_out)
        if not reverse:
            y = y + d_ref[:, cs] * u[:, cs]
        y_ref[:, cs] = y


def _s5_scan(xs, mod, layer, g, dvec, abar, bw, cw, n_ctx_steps, reverse):
    rows, d = xs.shape
    tile = SCAN_STEPS * SUBLANES
    n_tiles = rows // tile
    n_ctx = n_ctx_steps // SCAN_STEPS
    n_states = abar.shape[-1]

    n_lat = n_tiles - n_ctx

    def tile_of(k):
        if reverse:
            return n_tiles - 1 - k
        return jnp.where(k < n_ctx, n_lat + k, k - n_ctx)

    def is_ctx(k):
        return jnp.where(k < n_ctx, 1, 0)

    return pl.pallas_call(
        functools.partial(_s5_scan_kernel, steps=SCAN_STEPS, reverse=reverse),
        grid=(n_tiles,),
        in_specs=[
            pl.BlockSpec((tile, d), lambda k: (tile_of(k), 0)),
            pl.BlockSpec((None, None, SUBLANES, 2 * d), lambda k: (layer, is_ctx(k), 0, 0)),
            pl.BlockSpec((1, d), lambda k: (0, 0)),
            pl.BlockSpec((1, d), lambda k: (0, 0)),
            pl.BlockSpec(abar.shape, lambda k: (0, 0, 0)),
            pl.BlockSpec(bw.shape, lambda k: (0, 0, 0)),
            pl.BlockSpec(cw.shape, lambda k: (0, 0, 0)),
        ],
        out_specs=pl.BlockSpec((tile, d), lambda k: (tile_of(k), 0)),
        out_shape=jax.ShapeDtypeStruct((rows, d), F32),
        scratch_shapes=[
            pltpu.VMEM((tile, 2 * n_states), F32),
            pltpu.VMEM((2, SUBLANES, n_states), F32),
        ],
        compiler_params=_cparams(("arbitrary",)),
        name="s5_scan_bwd" if reverse else "s5_scan_fwd",
    )(xs, mod, g, dvec, abar, bw, cw)


def _mlp_sublayer(x, mod, g, w1_ref, w2_ref):
    d = x.shape[1]
    h = _norm_mod(x, g, mod[:, 3 * d:4 * d], mod[:, 4 * d:5 * d]).astype(BF16)
    acc = jnp.zeros(x.shape, F32)
    for c in range(w1_ref.shape[1] // d):
        cs = slice(c * d, (c + 1) * d)
        a = jnp.dot(h, w1_ref[:, cs], preferred_element_type=F32)
        a = jnp.square(jnp.maximum(a, 0.0)).astype(BF16)
        acc = acc + jnp.dot(a, w2_ref[cs, :], preferred_element_type=F32)
    return x + _per_batch(mod[:, 5 * d:], x.shape[0]) * acc


def _s5_tail_kernel(yf_ref, yb_ref, x_ref, mod_ref, g_ref, wg_ref, w1_ref, w2_ref, o_ref):
    d = x_ref.shape[1]
    mod = mod_ref[...]
    y = yf_ref[...] + yb_ref[...]
    z = jnp.dot(jax.nn.gelu(y).astype(BF16), wg_ref[...], preferred_element_type=F32)
    glu = z[:, :d] * jax.nn.sigmoid(z[:, d:])
    x1 = x_ref[...] + _per_batch(mod[:, 2 * d:3 * d], glu.shape[0]) * glu
    o_ref[...] = _mlp_sublayer(x1, mod, g_ref[...], w1_ref, w2_ref)


def _attn_tail_kernel(*refs, n_lat_tiles, has_ctx):
    if has_ctx:
        ol_ref, oc_ref, x_ref, mod_ref, g_ref, pm_ref, wo_ref, w1_ref, w2_ref, y_ref = refs
        o = jnp.where(pl.program_id(0) < n_lat_tiles, ol_ref[...], oc_ref[...])
    else:
        ol_ref, x_ref, mod_ref, g_ref, pm_ref, wo_ref, w1_ref, w2_ref, y_ref = refs
        o = ol_ref[...]
    rows, d = x_ref.shape
    mod = mod_ref[...]
    o = o.reshape(rows, -1)
    o = jnp.dot(pm_ref[...], o, preferred_element_type=F32).astype(BF16)
    y = jnp.dot(o, wo_ref[...], preferred_element_type=F32)
    x1 = x_ref[...] + _per_batch(mod[:, 2 * d:3 * d], rows) * y
    y_ref[...] = _mlp_sublayer(x1, mod, g_ref[...], w1_ref, w2_ref)


def _resident(arr):
    return pl.BlockSpec(arr.shape, lambda i: (0,) * arr.ndim, pipeline_mode=pl.Buffered(1))


def _s5_tail(yf, yb, xs, mod, layer, g, w_glu, w1, w2, n_lat_tiles):
    rows, d = xs.shape
    row_spec = pl.BlockSpec((ROW_TILE, d), lambda i: (i, 0))
    return pl.pallas_call(
        _s5_tail_kernel,
        grid=(rows // ROW_TILE,),
        in_specs=[
            row_spec, row_spec, row_spec,
            pl.BlockSpec((None, None, SUBLANES, mod.shape[-1]),
                         lambda i: (layer, jnp.where(i < n_lat_tiles, 0, 1), 0, 0)),
            _resident(g), _resident(w_glu), _resident(w1), _resident(w2),
        ],
        out_specs=row_spec,
        out_shape=jax.ShapeDtypeStruct((rows, d), F32),
        compiler_params=_cparams(("arbitrary",)),
        name="s5_tail",
    )(yf, yb, xs, mod, g, w_glu, w1, w2)


def _qkv_kernel(x_ref, mod_ref, g_ref, pm_ref, w_ref, gain_ref, cos_ref, sin_ref,
                q_ref, k_ref, v_ref, *, head_dim):
    d_model = x_ref.shape[1]
    n_batch, steps, nq = q_ref.shape
    nk = k_ref.shape[2]
    mod = mod_ref[...]
    h = _norm_mod(x_ref[...], g_ref[...], mod[:, :d_model], mod[:, d_model:]).astype(BF16)
    h = jnp.dot(pm_ref[...], h, preferred_element_type=F32).astype(BF16)
    qkv = jnp.dot(h, w_ref[...], preferred_element_type=F32)
    rows = qkv.shape[0]

    lane = lax.broadcasted_iota(jnp.int32, (rows, LANES), 1)
    low_head = lane < head_dim
    first_half = (lane & (head_dim // 4)) == 0
    cos = jnp.tile(cos_ref[...], (n_batch, 1))
    sin = jnp.tile(sin_ref[...], (n_batch, 1))

    def put(ref, cs, val):
        for b in range(n_batch):
            ref[b, :, cs] = val[b * steps:(b + 1) * steps]

    def norm_rope(blk, gain):
        sq = blk * blk
        lo = jnp.sum(jnp.where(low_head, sq, 0.0), axis=-1, keepdims=True)
        hi = jnp.sum(jnp.where(low_head, 0.0, sq), axis=-1, keepdims=True)
        r = lax.rsqrt(jnp.where(low_head, lo, hi) * (1.0 / head_dim) + NORM_EPS)
        xn = (blk * r) * gain
        pair = jnp.where(first_half,
                         pltpu.roll(xn, LANES - head_dim // 4, axis=1),
                         pltpu.roll(xn, head_dim // 4, axis=1))
        return xn * cos + pair * sin

    for j in range(nq // LANES):
        cs = slice(j * LANES, (j + 1) * LANES)
        put(q_ref, cs, norm_rope(qkv[:, cs], gain_ref[0:1, :]).astype(BF16))
    for j in range(nk // LANES):
        cs = slice(j * LANES, (j + 1) * LANES)
        put(k_ref, cs, norm_rope(qkv[:, nq + j * LANES:nq + (j + 1) * LANES],
                                 gain_ref[1:2, :]).astype(BF16))
    for j in range(nk // LANES):
        vv = qkv[:, nq + nk + j * LANES:nq + nk + (j + 1) * LANES]
        put(v_ref, slice(2 * j * LANES, (2 * j + 1) * LANES),
            jnp.where(low_head, vv, 1.0).astype(BF16))
        put(v_ref, slice((2 * j + 1) * LANES, (2 * j + 2) * LANES),
            jnp.where(low_head, 1.0, vv).astype(BF16))


def _qkv(xs, mod, layer, g, w, gains, cos, sin, n_lat_tiles, head_dim):
    rows, d = xs.shape
    nq = N_HEADS * head_dim
    nk = N_KV_HEADS * LANES
    steps = ROW_TILE // SUBLANES
    t_all = rows // SUBLANES
    pm = _row_perm(ROW_TILE, to_batch_major=True)

    def is_ctx(i):
        return jnp.where(i < n_lat_tiles, 0, 1)

    return pl.pallas_call(
        functools.partial(_qkv_kernel, head_dim=head_dim),
        grid=(rows // ROW_TILE,),
        in_specs=[
            pl.BlockSpec((ROW_TILE, d), lambda i: (i, 0)),
            pl.BlockSpec((None, None, SUBLANES, 2 * d), lambda i: (layer, is_ctx(i), 0, 0)),
            pl.BlockSpec((1, d), lambda i: (0, 0)),
            pl.BlockSpec(pm.shape, lambda i: (0, 0)),
            pl.BlockSpec(w.shape, lambda i: (0, 0)),
            pl.BlockSpec(gains.shape, lambda i: (0, 0)),
            pl.BlockSpec((steps, LANES), lambda i: (i, 0)),
            pl.BlockSpec((steps, LANES), lambda i: (i, 0)),
        ],
        out_specs=[
            pl.BlockSpec((SUBLANES, steps, nq), lambda i: (0, i, 0)),
            pl.BlockSpec((SUBLANES, steps, nk), lambda i: (0, i, 0)),
            pl.BlockSpec((SUBLANES, steps, 2 * nk), lambda i: (0, i, 0)),
        ],
        out_shape=[
            jax.ShapeDtypeStruct((SUBLANES, t_all, nq), BF16),
            jax.ShapeDtypeStruct((SUBLANES, t_all, nk), BF16),
            jax.ShapeDtypeStruct((SUBLANES, t_all, 2 * nk), BF16),
        ],
        compiler_params=_cparams(("arbitrary",)),
        name="attn_qkv",
    )(xs, mod, g, pm, w, gains, cos, sin)


def _attn_kernel(q_ref, k_ref, v_ref, o_ref, *, head_dim):
    tq, nq = q_ref.shape
    n_heads = nq // head_dim
    q_per_kv = n_heads // (k_ref.shape[1] // LANES)
    lane = lax.broadcasted_iota(jnp.int32, (tq, LANES), 1)
    low_head = lane < head_dim

    def scores(h):
        j, hi_head = divmod(h, 2)
        kv = h // q_per_kv
        qblk = q_ref[:, j * LANES:(j + 1) * LANES].astype(F32)
        keep = jnp.logical_not(low_head) if hi_head else low_head
        qh = jnp.where(keep, qblk, 0.0).astype(BF16)
        k = k_ref[:, kv * LANES:(kv + 1) * LANES]
        return lax.dot_general(qh, k, (((1,), (1,)), ((), ())), preferred_element_type=F32)

    ahead = [scores(h) for h in range(min(SCORE_LOOKAHEAD, n_heads))]
    lo = None
    for h in range(n_heads):
        s = ahead.pop(0)
        if h + SCORE_LOOKAHEAD < n_heads:
            ahead.append(scores(h + SCORE_LOOKAHEAD))
        e = jnp.exp2(s - jnp.max(s, axis=-1, keepdims=True)).astype(BF16)
        vcol = (2 * (h // q_per_kv) + h % 2) * LANES
        res = jnp.dot(e, v_ref[:, vcol:vcol + LANES], preferred_element_type=F32)
        if h % 2 == 0:
            lo = res
        else:
            num = jnp.where(low_head, lo, res)
            den = pltpu.roll(jnp.where(low_head, res, lo), head_dim, axis=1)
            j = h // 2
            o_ref[:, j * LANES:(j + 1) * LANES] = (num / den).astype(BF16)


def _attention(q, k2, v2, q_start, q_len, q_tile, key_start, key_len, head_dim):
    n_batch, _, nq = q.shape
    assert q_start % q_tile == 0 and q_len % q_tile == 0 and key_start % key_len == 0
    q0 = q_start // q_tile
    k0 = key_start // key_len
    return pl.pallas_call(
        functools.partial(_attn_kernel, head_dim=head_dim),
        grid=(n_batch, q_len // q_tile),
        in_specs=[
            pl.BlockSpec((None, q_tile, nq), lambda b, i: (b, i + q0, 0)),
            pl.BlockSpec((None, key_len, k2.shape[2]), lambda b, i: (b, k0, 0)),
            pl.BlockSpec((None, key_len, v2.shape[2]), lambda b, i: (b, k0, 0)),
        ],
        out_specs=pl.BlockSpec((None, q_tile, nq), lambda b, i: (b, i, 0)),
        out_shape=jax.ShapeDtypeStruct((n_batch, q_len, nq), BF16),
        compiler_params=_cparams(("arbitrary", "arbitrary")),
        name="attention",
    )(q, k2, v2)


def _attn_tail(o_lat, o_ctx, xs, mod, layer, g, w_o, w1, w2):
    n_batch, n_lat, d = o_lat.shape
    steps = ROW_TILE // n_batch
    n_lat_tiles = n_lat // steps
    n_tiles = n_lat_tiles + (o_ctx.shape[1] // steps if o_ctx is not None else 0)
    pm = _row_perm(ROW_TILE, to_batch_major=False)
    o_specs = [pl.BlockSpec((n_batch, steps, d), lambda i: (0, jnp.minimum(i, n_lat_tiles - 1), 0))]
    if o_ctx is not None:
        o_specs.append(
            pl.BlockSpec((n_batch, steps, d), lambda i: (0, jnp.maximum(i - n_lat_tiles, 0), 0)))
    return pl.pallas_call(
        functools.partial(_attn_tail_kernel, n_lat_tiles=n_lat_tiles, has_ctx=o_ctx is not None),
        grid=(n_tiles,),
        in_specs=o_specs + [
            pl.BlockSpec((ROW_TILE, d), lambda i: (i, 0)),
            pl.BlockSpec((None, None, SUBLANES, mod.shape[-1]),
                         lambda i: (layer, jnp.where(i < n_lat_tiles, 0, 1), 0, 0)),
            _resident(g), _resident(pm), _resident(w_o), _resident(w1), _resident(w2),
        ],
        out_specs=pl.BlockSpec((ROW_TILE, d), lambda i: (i, 0)),
        out_shape=jax.ShapeDtypeStruct((n_tiles * ROW_TILE, d), F32),
        compiler_params=_cparams(("arbitrary",)),
        name="attn_tail",
    )(*([o_lat] + ([o_ctx] if o_ctx is not None else [])), xs, mod, g, pm, w_o, w1, w2)


def _final_kernel(x_ref, g_ref, pm_ref, o_ref):
    x = x_ref[...]
    y = (x * _rms(x)) * g_ref[...]
    o_ref[...] = _permute_exact(pm_ref[...], y).reshape(o_ref.shape)


def _final_norm(xs, g, n_batch, seq):
    d = xs.shape[1]
    steps = ROW_TILE // n_batch
    pm = _row_perm(ROW_TILE, to_batch_major=True)
    return pl.pallas_call(
        _final_kernel,
        grid=(seq // steps,),
        in_specs=[
            pl.BlockSpec((ROW_TILE, d), lambda j: (j, 0)),
            pl.BlockSpec((1, d), lambda j: (0, 0)),
            pl.BlockSpec(pm.shape, lambda j: (0, 0)),
        ],
        out_specs=pl.BlockSpec((n_batch, steps, d), lambda j: (0, j, 0)),
        out_shape=jax.ShapeDtypeStruct((n_batch, seq, d), F32),
        compiler_params=_cparams(("arbitrary",)),
        name="final_norm",
    )(xs, g, pm)


def _block_diag(blocks, per_block):
    g, r, c = blocks.shape
    nb = g // per_block
    eye = jnp.eye(per_block, dtype=blocks.dtype)
    out = jnp.einsum('ngrc,gk->ngrkc', blocks.reshape(nb, per_block, r, c), eye)
    return out.reshape(nb, per_block * r, per_block * c)


def _s5_params(a_re, a_im, log_dt, b_re, b_im, c_re, c_im):
    per_block = S5_OUT_BLOCK // S5_GROUP
    out = []
    for dirn in range(2):
        ar = a_re[dirn].astype(F32)
        ai = a_im[dirn].astype(F32)
        dt = jnp.exp(log_dt[dirn].astype(F32))[:, None]
        mag = jnp.exp(dt * ar)
        abar_re = mag * jnp.cos(dt * ai)
        abar_im = mag * jnp.sin(dt * ai)
        den = ar * ar + ai * ai
        nr = abar_re - 1.0
        f_re = (nr * ar + abar_im * ai) / den
        f_im = (abar_im * ar - nr * ai) / den
        br = b_re[dirn].astype(F32)
        bi = b_im[dirn].astype(F32)
        bb_re = f_re[..., None] * br - f_im[..., None] * bi
        bb_im = f_re[..., None] * bi + f_im[..., None] * br
        bw = jnp.concatenate([_block_diag(jnp.swapaxes(bb_re, 1, 2), per_block),
                              _block_diag(jnp.swapaxes(bb_im, 1, 2), per_block)], axis=-1)
        bw = jnp.swapaxes(_block_diag(bw, S5_IN_BLOCK // S5_OUT_BLOCK), 1, 2)
        cw = jnp.concatenate([_block_diag(jnp.swapaxes(c_re[dirn].astype(F32), 1, 2), per_block),
                              _block_diag(-jnp.swapaxes(c_im[dirn].astype(F32), 1, 2), per_block)],
                             axis=1)
        ab = jnp.stack([abar_re.reshape(-1), abar_im.reshape(-1)])
        ab = jnp.broadcast_to(ab[:, None, :], (2, SUBLANES, ab.shape[-1]))
        out.append((ab, bw.astype(BF16), cw.astype(BF16)))
    return out


def _rope_tables(n_ctx, seq, head_dim):
    axis_dim = head_dim // 2
    n_freq = axis_dim // 2
    pos = jnp.arange(seq, dtype=jnp.int32)
    row = (pos // GRID_W).astype(F32)
    col = (pos % GRID_W).astype(F32)
    inv_freq = ROPE_THETA ** (-jnp.arange(n_freq, dtype=F32) / n_freq)
    ang_row = row[:, None] * inv_freq[None, :]
    ang_col = col[:, None] * inv_freq[None, :]
    cos = jnp.concatenate([jnp.cos(ang_row)] * 2 + [jnp.cos(ang_col)] * 2, axis=-1)
    sin = jnp.concatenate([-jnp.sin(ang_row), jnp.sin(ang_row),
                           -jnp.sin(ang_col), jnp.sin(ang_col)], axis=-1)
    cos = jnp.concatenate([cos, jnp.ones((n_ctx, head_dim), F32)], axis=0)
    sin = jnp.concatenate([sin, jnp.zeros((n_ctx, head_dim), F32)], axis=0)
    per_block = LANES // head_dim
    return jnp.tile(cos, (1, per_block)), jnp.tile(sin, (1, per_block))


def _qkv_params(w_qkv, q_g, k_g, head_dim):
    d = w_qkv.shape[0]
    nq = N_HEADS * head_dim
    nkv = N_KV_HEADS * head_dim
    per_block = LANES // head_dim

    def dup(w):
        w = w.reshape(d, N_KV_HEADS, 1, head_dim)
        return jnp.broadcast_to(w, (d, N_KV_HEADS, per_block, head_dim)).reshape(d, -1)

    w = jnp.concatenate([w_qkv[:, :nq], dup(w_qkv[:, nq:nq + nkv]), dup(w_qkv[:, nq + nkv:])], axis=1)
    q_scale = (head_dim ** -0.5) * LOG2_E
    gains = jnp.stack([jnp.tile(q_g * q_scale, per_block), jnp.tile(k_g, per_block)])
    return w.astype(BF16), gains


def kernel(x, c, ctx, c_ctx, ada_w, ada_b, norm_mix_g, norm_ffn_g, s5_a_re, s5_a_im, s5_log_dt,
           s5_b_re, s5_b_im, s5_c_re, s5_c_im, s5_d, s5_w_glu, attn_w_qkv, attn_q_g, attn_k_g,
           attn_w_o, ffn_w1, ffn_w2, final_g):
    n_batch, seq, d = x.shape
    n_ctx = ctx.shape[1]
    depth = ada_w.shape[0]
    head_dim = d // N_HEADS
    t_all = n_ctx + seq
    assert n_batch == SUBLANES and seq % Q_TILE == 0 and seq % n_ctx == 0
    assert n_ctx % SCAN_STEPS == 0 and seq % SCAN_STEPS == 0 and d % S5_IN_BLOCK == 0
    lat_row_tiles = seq * n_batch // ROW_TILE

    xs = _to_stream(x, ctx)
    c2 = jnp.concatenate([c, jnp.broadcast_to(c_ctx[None, :], (SUBLANES, d))], axis=0)
    mod = _adaln(c2, ada_w, ada_b).reshape(depth, 2, SUBLANES, 6 * d)
    cos, sin = _rope_tables(n_ctx, seq, head_dim)

    for i in range(depth):
        last = i == depth - 1
        j = i // 2
        g_mix = norm_mix_g[i].reshape(1, d)
        g_ffn = norm_ffn_g[i].reshape(1, d)
        w1 = ffn_w1[i].astype(BF16)
        w2 = ffn_w2[i].astype(BF16)
        if i % 2 == 0:
            fwd, bwd = _s5_params(s5_a_re[j], s5_a_im[j], s5_log_dt[j], s5_b_re[j], s5_b_im[j],
                                  s5_c_re[j], s5_c_im[j])
            dvec = s5_d[j].reshape(1, d)
            yf = _s5_scan(xs, mod, i, g_mix, dvec, *fwd, n_ctx, reverse=False)
            yb = _s5_scan(xs, mod, i, g_mix, dvec, *bwd, n_ctx, reverse=True)
            xs = _s5_tail(yf, yb, xs, mod, i, g_ffn, s5_w_glu[j].astype(BF16), w1, w2, lat_row_tiles)
        else:
            w, gains = _qkv_params(attn_w_qkv[j], attn_q_g[j], attn_k_g[j], head_dim)
            q, k2, v2 = _qkv(xs, mod, i, g_mix, w, gains, cos, sin, lat_row_tiles, head_dim)
            o_lat = _attention(q, k2, v2, 0, seq, Q_TILE, 0, t_all, head_dim)
            o_ctx = None if last else _attention(q, k2, v2, seq, n_ctx, n_ctx, seq, n_ctx, head_dim)
            xs = _attn_tail(o_lat, o_ctx, xs, mod, i, g_ffn, attn_w_o[j].astype(BF16), w1, w2)

    return _final_norm(xs, final_g.reshape(1, d), n_batch, seq)
```

```python
import functools

import numpy as np
import jax
import jax.numpy as jnp
from jax import lax
from jax.experimental import pallas as pl
from jax.experimental.pallas import tpu as pltpu

F32 = jnp.float32
BF16 = jnp.bfloat16

GRID_W = 64
N_HEADS = 16
N_KV_HEADS = 4
Q_PER_KV = N_HEADS // N_KV_HEADS
S5_GROUP = 16
ROPE_THETA = 10000.0
NORM_EPS = 1e-6
LOG2_E = 1.4426950408889634

SUBLANES = 8
LANES = 128
MXU_DIM = 256

ROW_TILE = 256
S5_TAIL_TILE = 512
SCAN_STEPS = 32
Q_TILE = 512
SCORE_LOOKAHEAD = 2
S5_IN_BLOCK = LANES
S5_OUT_BLOCK = LANES
VMEM_LIMIT = 48 * 1024 * 1024


def _cparams(semantics):
    return pltpu.CompilerParams(dimension_semantics=semantics, vmem_limit_bytes=VMEM_LIMIT)


def _per_batch(v, rows):
    return jnp.tile(v, (rows // SUBLANES, 1))


def _rms(x):
    return lax.rsqrt(jnp.mean(x * x, axis=-1, keepdims=True) + NORM_EPS)


def _norm_mod(x, g, shift, scale):
    h = (x * _rms(x)) * g
    return h * _per_batch(1.0 + scale, x.shape[0]) + _per_batch(shift, x.shape[0])


def _row_perm(rows, to_batch_major):
    nt = rows // SUBLANES
    r = np.arange(rows)
    if to_batch_major:
        src = (r % nt) * SUBLANES + r // nt
    else:
        src = (r % SUBLANES) * nt + r // SUBLANES
    return jnp.asarray(np.equal(src[:, None], r[None, :]), dtype=BF16)


def _permute_exact(pm, x):
    hi = x.astype(BF16)
    r1 = x - hi.astype(F32)
    mid = r1.astype(BF16)
    lo = (r1 - mid.astype(F32)).astype(BF16)
    return (jnp.dot(pm, hi, preferred_element_type=F32)
            + jnp.dot(pm, mid, preferred_element_type=F32)
            + jnp.dot(pm, lo, preferred_element_type=F32))


def _to_stream_kernel(pm_ref, x_ref, ctx_ref, o_ref, *, n_lat_tiles):
    j = pl.program_id(0)
    rows = o_ref.shape[0]

    @pl.when(j < n_lat_tiles)
    def _():
        o_ref[...] = _permute_exact(pm_ref[...], x_ref[...].reshape(rows, -1))

    @pl.when(j >= n_lat_tiles)
    def _():
        o_ref[...] = _permute_exact(pm_ref[...], ctx_ref[...].reshape(rows, -1))


def _to_stream(x, ctx):
    b, l, d = x.shape
    cl = ctx.shape[1]
    steps = ROW_TILE // b
    n_lat_tiles = l // steps
    pm = _row_perm(ROW_TILE, to_batch_major=False)
    return pl.pallas_call(
        functools.partial(_to_stream_kernel, n_lat_tiles=n_lat_tiles),
        grid=((cl + l) // steps,),
        in_specs=[
            pl.BlockSpec(pm.shape, lambda j: (0, 0)),
            pl.BlockSpec((b, steps, d), lambda j: (0, jnp.minimum(j, n_lat_tiles - 1), 0)),
            pl.BlockSpec((b, steps, d), lambda j: (0, jnp.maximum(j - n_lat_tiles, 0), 0)),
        ],
        out_specs=pl.BlockSpec((ROW_TILE, d), lambda j: (j, 0)),
        out_shape=jax.ShapeDtypeStruct(((cl + l) * b, d), F32),
        compiler_params=_cparams(("arbitrary",)),
        name="to_stream",
    )(pm, x, ctx)


def _adaln_kernel(c_ref, w_ref, b_ref, o_ref):
    s = jax.nn.silu(c_ref[...]).astype(BF16)
    o_ref[...] = jnp.dot(s, w_ref[...].astype(BF16), preferred_element_type=F32) + b_ref[...]


def _adaln(c2, ada_w, ada_b):
    depth, d, n = ada_w.shape
    tn = n // 4
    return pl.pallas_call(
        _adaln_kernel,
        grid=(depth, n // tn),
        in_specs=[
            pl.BlockSpec(c2.shape, lambda i, j: (0, 0)),
            pl.BlockSpec((None, d, tn), lambda i, j: (i, 0, j)),
            pl.BlockSpec((None, 1, tn), lambda i, j: (i, 0, j)),
        ],
        out_specs=pl.BlockSpec((None, c2.shape[0], tn), lambda i, j: (i, 0, j)),
        out_shape=jax.ShapeDtypeStruct((depth, c2.shape[0], n), F32),
        compiler_params=_cparams(("arbitrary", "arbitrary")),
        name="adaln",
    )(c2, ada_w, ada_b.reshape(depth, 1, n))


def _s5_scan_kernel(x_ref, mod_ref, g_ref, d_ref, a_ref, bw_ref, cw_ref, y_ref, bu_ref, st_ref,
                    *, steps, reverse):
    d_model = x_ref.shape[1]
    n_in, st_in, ch_in = bw_ref.shape
    n_out, st_out, ch_out = cw_ref.shape
    half = st_out // 2

    @pl.when(pl.program_id(0) == 0)
    def _():
        st_ref[...] = jnp.zeros_like(st_ref)

    mod = mod_ref[...]
    u = _norm_mod(x_ref[...], g_ref[...], mod[:, :d_model], mod[:, d_model:])
    ub = u.astype(BF16)

    for j in range(n_in):
        bu_ref[:, j * st_in:(j + 1) * st_in] = lax.dot_general(
            ub[:, j * ch_in:(j + 1) * ch_in], bw_ref[j], (((1,), (1,)), ((), ())),
            preferred_element_type=F32)

    order = range(steps - 1, -1, -1) if reverse else range(steps)
    for c in range(n_out):
        re = slice(c * st_out, c * st_out + half)
        im = slice(c * st_out + half, (c + 1) * st_out)
        sc = slice(c * half, (c + 1) * half)
        ar = a_ref[0, :, sc]
        ai = a_ref[1, :, sc]
        hr = st_ref[0, :, sc]
        hi = st_ref[1, :, sc]
        for t in order:
            rows = slice(t * SUBLANES, (t + 1) * SUBLANES)
            nr = ar * hr - ai * hi + bu_ref[rows, re]
            ni = ar * hi + ai * hr + bu_ref[rows, im]
            bu_ref[rows, re] = nr
            bu_ref[rows, im] = ni
            hr, hi = nr, ni
        st_ref[0, :, sc] = hr
        st_ref[1, :, sc] = hi

        hb = bu_ref[:, c * st_out:(c + 1) * st_out].astype(BF16)
        y = jnp.dot(hb, cw_ref[c], preferred_element_type=F32)
        cs = slice(c * ch_out, (c + 1) * ch_out)
        if not reverse:
            y = y + d_ref[:, cs] * u[:, cs]
        y_ref[:, cs] = y


def _s5_scan(xs, mod, layer, g, dvec, abar, bw, cw, s5_layer, n_ctx_steps, reverse):
    rows, d = xs.shape
    tile = SCAN_STEPS * SUBLANES
    n_tiles = rows // tile
    n_ctx = n_ctx_steps // SCAN_STEPS
    n_states = abar.shape[-1]
    n_lat = n_tiles - n_ctx
    dirn = int(reverse)

    def picked(arr):
        return pl.BlockSpec((None, None) + arr.shape[2:],
                            lambda k: (s5_layer, dirn) + (0,) * (arr.ndim - 2),
                            pipeline_mode=pl.Buffered(1))

    def tile_of(k):
        if reverse:
            return n_tiles - 1 - k
        return jnp.where(k < n_ctx, n_lat + k, k - n_ctx)

    def is_ctx(k):
        return jnp.where(k < n_ctx, 1, 0)

    return pl.pallas_call(
        functools.partial(_s5_scan_kernel, steps=SCAN_STEPS, reverse=reverse),
        grid=(n_tiles,),
        in_specs=[
            pl.BlockSpec((tile, d), lambda k: (tile_of(k), 0)),
            pl.BlockSpec((None, None, SUBLANES, 2 * d), lambda k: (layer, is_ctx(k), 0, 0)),
            pl.BlockSpec((1, d), lambda k: (0, 0)),
            pl.BlockSpec((1, d), lambda k: (0, 0)),
            picked(abar), picked(bw), picked(cw),
        ],
        out_specs=pl.BlockSpec((tile, d), lambda k: (tile_of(k), 0)),
        out_shape=jax.ShapeDtypeStruct((rows, d), F32),
        scratch_shapes=[
            pltpu.VMEM((tile, 2 * n_states), F32),
            pltpu.VMEM((2, SUBLANES, n_states), F32),
        ],
        compiler_params=_cparams(("arbitrary",)),
        name="s5_scan_bwd" if reverse else "s5_scan_fwd",
    )(xs, mod, g, dvec, abar, bw, cw)


def _mlp_sublayer(x, mod, g, w1_ref, w2_ref):
    d = x.shape[1]
    h = _norm_mod(x, g, mod[:, 3 * d:4 * d], mod[:, 4 * d:5 * d]).astype(BF16)
    acc = jnp.zeros(x.shape, F32)
    for c in range(w1_ref.shape[1] // d):
        cs = slice(c * d, (c + 1) * d)
        a = jnp.dot(h, w1_ref[:, cs], preferred_element_type=F32)
        a = jnp.square(jnp.maximum(a, 0.0)).astype(BF16)
        acc = acc + jnp.dot(a, w2_ref[cs, :], preferred_element_type=F32)
    return x + _per_batch(mod[:, 5 * d:], x.shape[0]) * acc


def _s5_tail_kernel(yf_ref, yb_ref, x_ref, mod_ref, g_ref, wg_ref, w1_ref, w2_ref, o_ref):
    d = x_ref.shape[1]
    mod = mod_ref[...]
    y = yf_ref[...] + yb_ref[...]
    z = jnp.dot(jax.nn.gelu(y).astype(BF16), wg_ref[...], preferred_element_type=F32)
    glu = z[:, :d] * jax.nn.sigmoid(z[:, d:])
    x1 = x_ref[...] + _per_batch(mod[:, 2 * d:3 * d], glu.shape[0]) * glu
    o_ref[...] = _mlp_sublayer(x1, mod, g_ref[...], w1_ref, w2_ref)


def _attn_tail_kernel(*refs, n_lat_tiles, has_ctx, final):
    refs = list(refs)
    ol_ref = refs.pop(0)
    o = ol_ref[...]
    if has_ctx:
        o = jnp.where(pl.program_id(0) < n_lat_tiles, o, refs.pop(0)[...])
    x_ref, mod_ref, g_ref, pm_ref, wo_ref, w1_ref, w2_ref = refs[:7]
    y_ref = refs[-1]
    rows, d = x_ref.shape
    mod = mod_ref[...]
    o = o.reshape(rows, -1)
    o = jnp.dot(pm_ref[...], o, preferred_element_type=F32).astype(BF16)
    y = jnp.dot(o, wo_ref[...], preferred_element_type=F32)
    x1 = x_ref[...] + _per_batch(mod[:, 2 * d:3 * d], rows) * y
    x2 = _mlp_sublayer(x1, mod, g_ref[...], w1_ref, w2_ref)
    if final:
        gf_ref, pmf_ref = refs[7:9]
        x2 = (x2 * _rms(x2)) * gf_ref[...]
        y_ref[...] = _permute_exact(pmf_ref[...], x2).reshape(y_ref.shape)
    else:
        y_ref[...] = x2


def _resident(arr, index=None):
    if index is None:
        return pl.BlockSpec(arr.shape, lambda i: (0,) * arr.ndim, pipeline_mode=pl.Buffered(1))
    return pl.BlockSpec((None,) + arr.shape[1:], lambda i: (index,) + (0,) * (arr.ndim - 1),
                        pipeline_mode=pl.Buffered(1))


def _s5_tail(yf, yb, xs, mod, layer, g, w_glu, s5_layer, w1, w2, n_lat_rows):
    rows, d = xs.shape
    row_spec = pl.BlockSpec((S5_TAIL_TILE, d), lambda i: (i, 0))
    n_lat_tiles = n_lat_rows // S5_TAIL_TILE
    return pl.pallas_call(
        _s5_tail_kernel,
        grid=(rows // S5_TAIL_TILE,),
        in_specs=[
            row_spec, row_spec, row_spec,
            pl.BlockSpec((None, None, SUBLANES, mod.shape[-1]),
                         lambda i: (layer, jnp.where(i < n_lat_tiles, 0, 1), 0, 0)),
            _resident(g), _resident(w_glu, s5_layer), _resident(w1, layer), _resident(w2, layer),
        ],
        out_specs=row_spec,
        out_shape=jax.ShapeDtypeStruct((rows, d), F32),
        compiler_params=_cparams(("arbitrary",)),
        name="s5_tail",
    )(yf, yb, xs, mod, g, w_glu, w1, w2)


def _qkv_kernel(x_ref, mod_ref, g_ref, pm_ref, w_ref, gain_ref, same_ref, cos_ref, sin_ref,
                q_ref, k_ref, v_ref, *, head_dim):
    d_model = x_ref.shape[1]
    n_batch, steps, nq = q_ref.shape
    nk = k_ref.shape[2]
    mod = mod_ref[...]
    h = _norm_mod(x_ref[...], g_ref[...], mod[:, :d_model], mod[:, d_model:]).astype(BF16)
    h = jnp.dot(pm_ref[...], h, preferred_element_type=F32).astype(BF16)
    qkv = jnp.dot(h, w_ref[...], preferred_element_type=F32)
    rows = qkv.shape[0]

    lane = lax.broadcasted_iota(jnp.int32, (rows, LANES), 1)
    low_half = lane < head_dim
    cos = jnp.tile(cos_ref[...], (n_batch, 1))
    sin = jnp.tile(sin_ref[...], (n_batch, 1))
    same_head = same_ref[...]

    def put(ref, cs, val):
        for b in range(n_batch):
            ref[b, :, cs] = val[b * steps:(b + 1) * steps]

    def norm_rope(blk, gain):
        ssum = jnp.dot((blk * blk).astype(BF16), same_head, preferred_element_type=F32)
        xn = (blk * lax.rsqrt(ssum * (1.0 / head_dim) + NORM_EPS)) * gain
        return xn * cos + pltpu.roll(xn, LANES // 2, axis=1) * sin

    for j in range(nq // LANES):
        cs = slice(j * LANES, (j + 1) * LANES)
        put(q_ref, cs, norm_rope(qkv[:, cs], gain_ref[0:1, :]).astype(BF16))
    for j in range(nk // LANES):
        cs = slice(j * LANES, (j + 1) * LANES)
        put(k_ref, cs, norm_rope(qkv[:, nq + j * LANES:nq + (j + 1) * LANES],
                                 gain_ref[1:2, :]).astype(BF16))
    for j in range(nk // LANES):
        vv = qkv[:, nq + nk + j * LANES:nq + nk + (j + 1) * LANES]
        put(v_ref, slice(2 * j * LANES, (2 * j + 1) * LANES),
            jnp.where(low_half, vv, 1.0).astype(BF16))
        put(v_ref, slice((2 * j + 1) * LANES, (2 * j + 2) * LANES),
            jnp.where(low_half, 1.0, vv).astype(BF16))


def _qkv(xs, mod, layer, g, w, gains, attn_layer, cos, sin, n_lat_tiles, head_dim):
    rows, d = xs.shape
    nq = N_HEADS * head_dim
    nk = N_KV_HEADS * LANES
    steps = ROW_TILE // SUBLANES
    t_all = rows // SUBLANES
    pm = _row_perm(ROW_TILE, to_batch_major=True)
    _, slot = _head_lane_layout(head_dim)
    same = jnp.asarray(np.equal(slot[:, None], slot[None, :]), dtype=BF16)

    def is_ctx(i):
        return jnp.where(i < n_lat_tiles, 0, 1)

    return pl.pallas_call(
        functools.partial(_qkv_kernel, head_dim=head_dim),
        grid=(rows // ROW_TILE,),
        in_specs=[
            pl.BlockSpec((ROW_TILE, d), lambda i: (i, 0)),
            pl.BlockSpec((None, None, SUBLANES, 2 * d), lambda i: (layer, is_ctx(i), 0, 0)),
            pl.BlockSpec((1, d), lambda i: (0, 0)),
            pl.BlockSpec(pm.shape, lambda i: (0, 0)),
            _resident(w, attn_layer),
            _resident(gains, attn_layer),
            pl.BlockSpec(same.shape, lambda i: (0, 0)),
            pl.BlockSpec((steps, LANES), lambda i: (i, 0)),
            pl.BlockSpec((steps, LANES), lambda i: (i, 0)),
        ],
        out_specs=[
            pl.BlockSpec((SUBLANES, steps, nq), lambda i: (0, i, 0)),
            pl.BlockSpec((SUBLANES, steps, nk), lambda i: (0, i, 0)),
            pl.BlockSpec((SUBLANES, steps, 2 * nk), lambda i: (0, i, 0)),
        ],
        out_shape=[
            jax.ShapeDtypeStruct((SUBLANES, t_all, nq), BF16),
            jax.ShapeDtypeStruct((SUBLANES, t_all, nk), BF16),
            jax.ShapeDtypeStruct((SUBLANES, t_all, 2 * nk), BF16),
        ],
        compiler_params=_cparams(("arbitrary",)),
        name="attn_qkv",
    )(xs, mod, g, pm, w, gains, same, cos, sin)


def _attn_kernel(q_ref, k_ref, v_ref, o_ref, *, head_dim):
    tq, nq = q_ref.shape
    n_heads = nq // head_dim
    q_per_kv = n_heads // (k_ref.shape[1] // LANES)
    lane = lax.broadcasted_iota(jnp.int32, (tq, LANES), 1)
    low_head = lane < head_dim
    first_slot = (lane & (head_dim // 2)) == 0

    def scores(h):
        j, hi_head = divmod(h, 2)
        kv = h // q_per_kv
        qblk = q_ref[:, j * LANES:(j + 1) * LANES].astype(F32)
        keep = jnp.logical_not(first_slot) if hi_head else first_slot
        qh = jnp.where(keep, qblk, 0.0).astype(BF16)
        k = k_ref[:, kv * LANES:(kv + 1) * LANES]
        return lax.dot_general(qh, k, (((1,), (1,)), ((), ())), preferred_element_type=F32)

    ahead = [scores(h) for h in range(min(SCORE_LOOKAHEAD, n_heads))]
    lo = None
    for h in range(n_heads):
        s = ahead.pop(0)
        if h + SCORE_LOOKAHEAD < n_heads:
            ahead.append(scores(h + SCORE_LOOKAHEAD))
        e = jnp.exp2(s - jnp.max(s, axis=-1, keepdims=True)).astype(BF16)
        vcol = (2 * (h // q_per_kv) + h % 2) * LANES
        res = jnp.dot(e, v_ref[:, vcol:vcol + LANES], preferred_element_type=F32)
        if h % 2 == 0:
            lo = res
        else:
            num = jnp.where(low_head, lo, res)
            den = pltpu.roll(jnp.where(low_head, res, lo), head_dim, axis=1)
            j = h // 2
            o_ref[:, j * LANES:(j + 1) * LANES] = (num / den).astype(BF16)


def _attention(q, k2, v2, q_start, q_len, q_tile, key_start, key_len, head_dim):
    n_batch, _, nq = q.shape
    assert q_start % q_tile == 0 and q_len % q_tile == 0 and key_start % key_len == 0
    q0 = q_start // q_tile
    k0 = key_start // key_len
    return pl.pallas_call(
        functools.partial(_attn_kernel, head_dim=head_dim),
        grid=(n_batch, q_len // q_tile),
        in_specs=[
            pl.BlockSpec((None, q_tile, nq), lambda b, i: (b, i + q0, 0)),
            pl.BlockSpec((None, key_len, k2.shape[2]), lambda b, i: (b, k0, 0)),
            pl.BlockSpec((None, key_len, v2.shape[2]), lambda b, i: (b, k0, 0)),
        ],
        out_specs=pl.BlockSpec((None, q_tile, nq), lambda b, i: (b, i, 0)),
        out_shape=jax.ShapeDtypeStruct((n_batch, q_len, nq), BF16),
        compiler_params=_cparams(("arbitrary", "arbitrary")),
        name="attention",
    )(q, k2, v2)


def _attn_tail(o_lat, o_ctx, xs, mod, layer, g, w_o, attn_layer, w1, w2, final_g=None):
    n_batch, n_lat, d = o_lat.shape
    steps = ROW_TILE // n_batch
    n_lat_tiles = n_lat // steps
    n_tiles = n_lat_tiles + (o_ctx.shape[1] // steps if o_ctx is not None else 0)
    pm = _row_perm(ROW_TILE, to_batch_major=False)
    operands = [o_lat]
    specs = [pl.BlockSpec((n_batch, steps, d), lambda i: (0, jnp.minimum(i, n_lat_tiles - 1), 0))]
    if o_ctx is not None:
        operands.append(o_ctx)
        specs.append(
            pl.BlockSpec((n_batch, steps, d), lambda i: (0, jnp.maximum(i - n_lat_tiles, 0), 0)))
    operands += [xs, mod, g, pm, w_o, w1, w2]
    specs += [
        pl.BlockSpec((ROW_TILE, d), lambda i: (i, 0)),
        pl.BlockSpec((None, None, SUBLANES, mod.shape[-1]),
                     lambda i: (layer, jnp.where(i < n_lat_tiles, 0, 1), 0, 0)),
        _resident(g), _resident(pm), _resident(w_o, attn_layer), _resident(w1, layer),
        _resident(w2, layer),
    ]
    if final_g is not None:
        assert o_ctx is None
        pmf = _row_perm(ROW_TILE, to_batch_major=True)
        operands += [final_g, pmf]
        specs += [_resident(final_g), _resident(pmf)]
        out_spec = pl.BlockSpec((n_batch, steps, d), lambda i: (0, i, 0))
        out_shape = jax.ShapeDtypeStruct((n_batch, n_lat, d), F32)
    else:
        out_spec = pl.BlockSpec((ROW_TILE, d), lambda i: (i, 0))
        out_shape = jax.ShapeDtypeStruct((n_tiles * ROW_TILE, d), F32)
    return pl.pallas_call(
        functools.partial(_attn_tail_kernel, n_lat_tiles=n_lat_tiles, has_ctx=o_ctx is not None,
                          final=final_g is not None),
        grid=(n_tiles,),
        in_specs=specs,
        out_specs=out_spec,
        out_shape=out_shape,
        compiler_params=_cparams(("arbitrary",)),
        name="attn_tail",
    )(*operands)


def _final_kernel(x_ref, g_ref, pm_ref, o_ref):
    x = x_ref[...]
    y = (x * _rms(x)) * g_ref[...]
    o_ref[...] = _permute_exact(pm_ref[...], y).reshape(o_ref.shape)


def _final_norm(xs, g, n_batch, seq):
    d = xs.shape[1]
    steps = ROW_TILE // n_batch
    pm = _row_perm(ROW_TILE, to_batch_major=True)
    return pl.pallas_call(
        _final_kernel,
        grid=(seq // steps,),
        in_specs=[
            pl.BlockSpec((ROW_TILE, d), lambda j: (j, 0)),
            pl.BlockSpec((1, d), lambda j: (0, 0)),
            pl.BlockSpec(pm.shape, lambda j: (0, 0)),
        ],
        out_specs=pl.BlockSpec((n_batch, steps, d), lambda j: (0, j, 0)),
        out_shape=jax.ShapeDtypeStruct((n_batch, seq, d), F32),
        compiler_params=_cparams(("arbitrary",)),
        name="final_norm",
    )(xs, g, pm)


def _block_diag(blocks, per_block):
    *lead, g, r, c = blocks.shape
    nb = g // per_block
    eye = jnp.eye(per_block, dtype=blocks.dtype)
    out = jnp.einsum('...ngrc,gk->...ngrkc', blocks.reshape(*lead, nb, per_block, r, c), eye)
    return out.reshape(*lead, nb, per_block * r, per_block * c)


def _s5_params(a_re, a_im, log_dt, b_re, b_im, c_re, c_im):
    per_block = S5_OUT_BLOCK // S5_GROUP
    ar = a_re.astype(F32)
    ai = a_im.astype(F32)
    dt = jnp.exp(log_dt.astype(F32))[..., None]
    mag = jnp.exp(dt * ar)
    abar_re = mag * jnp.cos(dt * ai)
    abar_im = mag * jnp.sin(dt * ai)
    den = ar * ar + ai * ai
    nr = abar_re - 1.0
    f_re = (nr * ar + abar_im * ai) / den
    f_im = (abar_im * ar - nr * ai) / den
    br = b_re.astype(F32)
    bi = b_im.astype(F32)
    bb_re = f_re[..., None] * br - f_im[..., None] * bi
    bb_im = f_re[..., None] * bi + f_im[..., None] * br
    assert S5_IN_BLOCK == S5_OUT_BLOCK
    bw = jnp.concatenate([_block_diag(bb_re, per_block), _block_diag(bb_im, per_block)], axis=-2)
    cw = jnp.concatenate([_block_diag(jnp.swapaxes(c_re.astype(F32), -1, -2), per_block),
                          _block_diag(-jnp.swapaxes(c_im.astype(F32), -1, -2), per_block)], axis=-2)
    lead = ar.shape[:2]
    ab = jnp.stack([abar_re.reshape(*lead, -1), abar_im.reshape(*lead, -1)], axis=2)
    ab = jnp.broadcast_to(ab[:, :, :, None, :], (*lead, 2, SUBLANES, ab.shape[-1]))
    return ab, bw.astype(BF16), cw.astype(BF16)


def _head_lane_layout(head_dim):
    quarter = head_dim // 4
    lane = np.arange(LANES)
    second = lane // (LANES // 2)
    pos = lane % (head_dim // 2)
    dim = np.where(pos < quarter, pos, 2 * quarter + (pos - quarter)) + quarter * second
    slot = (lane % (LANES // 2)) // (head_dim // 2)
    return dim, slot


def _rope_tables(n_ctx, seq, head_dim):
    quarter = head_dim // 4
    pos = jnp.arange(seq, dtype=jnp.int32)
    row = (pos // GRID_W).astype(F32)
    col = (pos % GRID_W).astype(F32)
    inv_freq = ROPE_THETA ** (-jnp.arange(quarter, dtype=F32) / quarter)
    ang = jnp.concatenate([row[:, None] * inv_freq[None, :], col[:, None] * inv_freq[None, :]],
                          axis=-1)
    ang = jnp.tile(ang, (1, LANES // ang.shape[1]))
    sign = jnp.where(jnp.arange(LANES) < LANES // 2, -1.0, 1.0)
    cos = jnp.concatenate([jnp.cos(ang), jnp.ones((n_ctx, LANES), F32)], axis=0)
    sin = jnp.concatenate([jnp.sin(ang) * sign, jnp.zeros((n_ctx, LANES), F32)], axis=0)
    return cos, sin


def _qkv_params(w_qkv, q_g, k_g, head_dim):
    nq = N_HEADS * head_dim
    nkv = N_KV_HEADS * head_dim
    per_block = LANES // head_dim
    dim, slot = _head_lane_layout(head_dim)
    q_cols = np.concatenate([(per_block * j + slot) * head_dim + dim
                             for j in range(N_HEADS // per_block)])
    k_cols = np.concatenate([nq + h * head_dim + dim for h in range(N_KV_HEADS)])
    v_cols = np.concatenate([nq + nkv + h * head_dim + np.arange(LANES) % head_dim
                             for h in range(N_KV_HEADS)])
    w = w_qkv[..., np.concatenate([q_cols, k_cols, v_cols])]
    q_scale = (head_dim ** -0.5) * LOG2_E
    gains = jnp.stack([q_g[..., dim] * q_scale, k_g[..., dim]], axis=-2)
    return w.astype(BF16), gains


def kernel(x, c, ctx, c_ctx, ada_w, ada_b, norm_mix_g, norm_ffn_g, s5_a_re, s5_a_im, s5_log_dt,
           s5_b_re, s5_b_im, s5_c_re, s5_c_im, s5_d, s5_w_glu, attn_w_qkv, attn_q_g, attn_k_g,
           attn_w_o, ffn_w1, ffn_w2, final_g):
    n_batch, seq, d = x.shape
    n_ctx = ctx.shape[1]
    depth = ada_w.shape[0]
    head_dim = d // N_HEADS
    t_all = n_ctx + seq
    assert n_batch == SUBLANES and seq % Q_TILE == 0 and seq % n_ctx == 0
    assert n_ctx % SCAN_STEPS == 0 and seq % SCAN_STEPS == 0 and d % S5_IN_BLOCK == 0
    lat_row_tiles = seq * n_batch // ROW_TILE

    xs = _to_stream(x, ctx)
    c2 = jnp.concatenate([c, jnp.broadcast_to(c_ctx[None, :], (SUBLANES, d))], axis=0)
    mod = _adaln(c2, ada_w, ada_b).reshape(depth, 2, SUBLANES, 6 * d)
    cos, sin = _rope_tables(n_ctx, seq, head_dim)

    w1 = ffn_w1.astype(BF16)
    w2 = ffn_w2.astype(BF16)
    w_glu = s5_w_glu.astype(BF16)
    w_o = attn_w_o.astype(BF16)
    s5_abar, s5_bw, s5_cw = _s5_params(s5_a_re, s5_a_im, s5_log_dt, s5_b_re, s5_b_im, s5_c_re, s5_c_im)
    w_qkv, qk_gains = _qkv_params(attn_w_qkv, attn_q_g, attn_k_g, head_dim)

    for i in range(depth):
        last = i == depth - 1
        j = i // 2
        g_mix = norm_mix_g[i].reshape(1, d)
        g_ffn = norm_ffn_g[i].reshape(1, d)
        if i % 2 == 0:
            dvec = s5_d[j].reshape(1, d)
            yf = _s5_scan(xs, mod, i, g_mix, dvec, s5_abar, s5_bw, s5_cw, j, n_ctx, reverse=False)
            yb = _s5_scan(xs, mod, i, g_mix, dvec, s5_abar, s5_bw, s5_cw, j, n_ctx, reverse=True)
            xs = _s5_tail(yf, yb, xs, mod, i, g_ffn, w_glu, j, w1, w2, seq * n_batch)
        else:
            q, k2, v2 = _qkv(xs, mod, i, g_mix, w_qkv, qk_gains, j, cos, sin, lat_row_tiles, head_dim)
            o_lat = _attention(q, k2, v2, 0, seq, Q_TILE, 0, t_all, head_dim)
            o_ctx = None if last else _attention(q, k2, v2, seq, n_ctx, n_ctx, seq, n_ctx, head_dim)
            xs = _attn_tail(o_lat, o_ctx, xs, mod, i, g_ffn, w_o, j, w1, w2,
                            final_g.reshape(1, d) if last else None)
            if last:
                return xs

    return _final_norm(xs, final_g.reshape(1, d), n_batch, seq)
```

```python
import functools

import numpy as np
import jax
import jax.numpy as jnp
from jax import lax
from jax.experimental import pallas as pl
from jax.experimental.pallas import tpu as pltpu

F32 = jnp.float32
BF16 = jnp.bfloat16

GRID_W = 64
N_HEADS = 16
N_KV_HEADS = 4
Q_PER_KV = N_HEADS // N_KV_HEADS
S5_GROUP = 16
ROPE_THETA = 10000.0
NORM_EPS = 1e-6
LOG2_E = 1.4426950408889634

SUBLANES = 8
LANES = 128
MXU_DIM = 256

ROW_TILE = 256
S5_TAIL_TILE = 512
SCAN_STEPS = 64
Q_TILE = 512
SCORE_LOOKAHEAD = 2
S5_IN_BLOCK = LANES
S5_OUT_BLOCK = LANES
VMEM_LIMIT = 48 * 1024 * 1024


def _cparams(semantics):
    return pltpu.CompilerParams(dimension_semantics=semantics, vmem_limit_bytes=VMEM_LIMIT)


def _per_batch(v, rows):
    return jnp.tile(v, (rows // SUBLANES, 1))


def _rms(x):
    return lax.rsqrt(jnp.mean(x * x, axis=-1, keepdims=True) + NORM_EPS)


def _norm_mod(x, g, shift, scale):
    h = (x * _rms(x)) * g
    return h * _per_batch(1.0 + scale, x.shape[0]) + _per_batch(shift, x.shape[0])


def _row_perm(rows, to_batch_major):
    nt = rows // SUBLANES
    r = np.arange(rows)
    if to_batch_major:
        src = (r % nt) * SUBLANES + r // nt
    else:
        src = (r % SUBLANES) * nt + r // SUBLANES
    return jnp.asarray(np.equal(src[:, None], r[None, :]), dtype=BF16)


def _permute_exact(pm, x):
    hi = x.astype(BF16)
    r1 = x - hi.astype(F32)
    mid = r1.astype(BF16)
    lo = (r1 - mid.astype(F32)).astype(BF16)
    return (jnp.dot(pm, hi, preferred_element_type=F32)
            + jnp.dot(pm, mid, preferred_element_type=F32)
            + jnp.dot(pm, lo, preferred_element_type=F32))


def _to_stream_kernel(pm_ref, x_ref, ctx_ref, o_ref, *, n_lat_tiles):
    j = pl.program_id(0)
    rows = o_ref.shape[0]

    @pl.when(j < n_lat_tiles)
    def _():
        o_ref[...] = _permute_exact(pm_ref[...], x_ref[...].reshape(rows, -1))

    @pl.when(j >= n_lat_tiles)
    def _():
        o_ref[...] = _permute_exact(pm_ref[...], ctx_ref[...].reshape(rows, -1))


def _to_stream(x, ctx):
    b, l, d = x.shape
    cl = ctx.shape[1]
    steps = ROW_TILE // b
    n_lat_tiles = l // steps
    pm = _row_perm(ROW_TILE, to_batch_major=False)
    return pl.pallas_call(
        functools.partial(_to_stream_kernel, n_lat_tiles=n_lat_tiles),
        grid=((cl + l) // steps,),
        in_specs=[
            pl.BlockSpec(pm.shape, lambda j: (0, 0)),
            pl.BlockSpec((b, steps, d), lambda j: (0, jnp.minimum(j, n_lat_tiles - 1), 0)),
            pl.BlockSpec((b, steps, d), lambda j: (0, jnp.maximum(j - n_lat_tiles, 0), 0)),
        ],
        out_specs=pl.BlockSpec((ROW_TILE, d), lambda j: (j, 0)),
        out_shape=jax.ShapeDtypeStruct(((cl + l) * b, d), F32),
        compiler_params=_cparams(("arbitrary",)),
        name="to_stream",
    )(pm, x, ctx)


def _adaln_kernel(c_ref, w_ref, b_ref, o_ref):
    s = jax.nn.silu(c_ref[...]).astype(BF16)
    o_ref[...] = jnp.dot(s, w_ref[...].astype(BF16), preferred_element_type=F32) + b_ref[...]


def _adaln(c2, ada_w, ada_b):
    depth, d, n = ada_w.shape
    tn = n // 4
    return pl.pallas_call(
        _adaln_kernel,
        grid=(depth, n // tn),
        in_specs=[
            pl.BlockSpec(c2.shape, lambda i, j: (0, 0)),
            pl.BlockSpec((None, d, tn), lambda i, j: (i, 0, j)),
            pl.BlockSpec((None, 1, tn), lambda i, j: (i, 0, j)),
        ],
        out_specs=pl.BlockSpec((None, c2.shape[0], tn), lambda i, j: (i, 0, j)),
        out_shape=jax.ShapeDtypeStruct((depth, c2.shape[0], n), F32),
        compiler_params=_cparams(("arbitrary", "arbitrary")),
        name="adaln",
    )(c2, ada_w, ada_b.reshape(depth, 1, n))


def _s5_scan_kernel(x_ref, mod_ref, g_ref, d_ref, a_ref, bw_ref, cw_ref, y_ref, bu_ref, st_ref,
                    *, steps, reverse):
    d_model = x_ref.shape[1]
    n_in, st_in, ch_in = bw_ref.shape
    n_out, st_out, ch_out = cw_ref.shape
    half = st_out // 2

    @pl.when(pl.program_id(0) == 0)
    def _():
        st_ref[...] = jnp.zeros_like(st_ref)

    mod = mod_ref[...]
    u = _norm_mod(x_ref[...], g_ref[...], mod[:, :d_model], mod[:, d_model:])
    ub = u.astype(BF16)

    for j in range(n_in):
        bu_ref[:, j * st_in:(j + 1) * st_in] = lax.dot_general(
            ub[:, j * ch_in:(j + 1) * ch_in], bw_ref[j], (((1,), (1,)), ((), ())),
            preferred_element_type=F32)

    order = range(steps - 1, -1, -1) if reverse else range(steps)
    for c in range(n_out):
        re = slice(c * st_out, c * st_out + half)
        im = slice(c * st_out + half, (c + 1) * st_out)
        sc = slice(c * half, (c + 1) * half)
        ar = a_ref[0, :, sc]
        ai = a_ref[1, :, sc]
        hr = st_ref[0, :, sc]
        hi = st_ref[1, :, sc]
        for t in order:
            rows = slice(t * SUBLANES, (t + 1) * SUBLANES)
            nr = ar * hr - ai * hi + bu_ref[rows, re]
            ni = ar * hi + ai * hr + bu_ref[rows, im]
            bu_ref[rows, re] = nr
            bu_ref[rows, im] = ni
            hr, hi = nr, ni
        st_ref[0, :, sc] = hr
        st_ref[1, :, sc] = hi

        hb = bu_ref[:, c * st_out:(c + 1) * st_out].astype(BF16)
        y = jnp.dot(hb, cw_ref[c], preferred_element_type=F32)
        cs = slice(c * ch_out, (c + 1) * ch_out)
        if not reverse:
            y = y + d_ref[:, cs] * u[:, cs]
        y_ref[:, cs] = y


def _s5_scan(xs, mod, layer, g, dvec, abar, bw, cw, s5_layer, n_ctx_steps, reverse):
    rows, d = xs.shape
    tile = SCAN_STEPS * SUBLANES
    n_tiles = rows // tile
    n_ctx = n_ctx_steps // SCAN_STEPS
    n_states = abar.shape[-1]
    n_lat = n_tiles - n_ctx
    dirn = int(reverse)

    def picked(arr):
        return pl.BlockSpec((None, None) + arr.shape[2:],
                            lambda k: (s5_layer, dirn) + (0,) * (arr.ndim - 2),
                            pipeline_mode=pl.Buffered(1))

    def tile_of(k):
        if reverse:
            return n_tiles - 1 - k
        return jnp.where(k < n_ctx, n_lat + k, k - n_ctx)

    def is_ctx(k):
        return jnp.where(k < n_ctx, 1, 0)

    return pl.pallas_call(
        functools.partial(_s5_scan_kernel, steps=SCAN_STEPS, reverse=reverse),
        grid=(n_tiles,),
        in_specs=[
            pl.BlockSpec((tile, d), lambda k: (tile_of(k), 0)),
            pl.BlockSpec((None, None, SUBLANES, 2 * d), lambda k: (layer, is_ctx(k), 0, 0)),
            pl.BlockSpec((1, d), lambda k: (0, 0)),
            pl.BlockSpec((1, d), lambda k: (0, 0)),
            picked(abar), picked(bw), picked(cw),
        ],
        out_specs=pl.BlockSpec((tile, d), lambda k: (tile_of(k), 0)),
        out_shape=jax.ShapeDtypeStruct((rows, d), F32),
        scratch_shapes=[
            pltpu.VMEM((tile, 2 * n_states), F32),
            pltpu.VMEM((2, SUBLANES, n_states), F32),
        ],
        compiler_params=_cparams(("arbitrary",)),
        name="s5_scan_bwd" if reverse else "s5_scan_fwd",
    )(xs, mod, g, dvec, abar, bw, cw)


def _mlp_sublayer(x, mod, g, w1_ref, w2_ref):
    d = x.shape[1]
    h = _norm_mod(x, g, mod[:, 3 * d:4 * d], mod[:, 4 * d:5 * d]).astype(BF16)
    acc = jnp.zeros(x.shape, F32)
    for c in range(w1_ref.shape[1] // d):
        cs = slice(c * d, (c + 1) * d)
        a = jnp.dot(h, w1_ref[:, cs], preferred_element_type=F32)
        a = jnp.square(jnp.maximum(a, 0.0)).astype(BF16)
        acc = acc + jnp.dot(a, w2_ref[cs, :], preferred_element_type=F32)
    return x + _per_batch(mod[:, 5 * d:], x.shape[0]) * acc


def _s5_tail_kernel(yf_ref, yb_ref, x_ref, mod_ref, g_ref, wg_ref, w1_ref, w2_ref, o_ref):
    d = x_ref.shape[1]
    mod = mod_ref[...]
    y = yf_ref[...] + yb_ref[...]
    z = jnp.dot(jax.nn.gelu(y).astype(BF16), wg_ref[...], preferred_element_type=F32)
    glu = z[:, :d] * jax.nn.sigmoid(z[:, d:])
    x1 = x_ref[...] + _per_batch(mod[:, 2 * d:3 * d], glu.shape[0]) * glu
    o_ref[...] = _mlp_sublayer(x1, mod, g_ref[...], w1_ref, w2_ref)


def _attn_tail_kernel(*refs, n_lat_tiles, has_ctx, final):
    refs = list(refs)
    ol_ref = refs.pop(0)
    o = ol_ref[...]
    if has_ctx:
        o = jnp.where(pl.program_id(0) < n_lat_tiles, o, refs.pop(0)[...])
    x_ref, mod_ref, g_ref, pm_ref, wo_ref, w1_ref, w2_ref = refs[:7]
    y_ref = refs[-1]
    rows, d = x_ref.shape
    mod = mod_ref[...]
    o = o.reshape(rows, -1)
    o = jnp.dot(pm_ref[...], o, preferred_element_type=F32).astype(BF16)
    y = jnp.dot(o, wo_ref[...], preferred_element_type=F32)
    x1 = x_ref[...] + _per_batch(mod[:, 2 * d:3 * d], rows) * y
    x2 = _mlp_sublayer(x1, mod, g_ref[...], w1_ref, w2_ref)
    if final:
        gf_ref, pmf_ref = refs[7:9]
        x2 = (x2 * _rms(x2)) * gf_ref[...]
        y_ref[...] = _permute_exact(pmf_ref[...], x2).reshape(y_ref.shape)
    else:
        y_ref[...] = x2


def _resident(arr, index=None):
    if index is None:
        return pl.BlockSpec(arr.shape, lambda i: (0,) * arr.ndim, pipeline_mode=pl.Buffered(1))
    return pl.BlockSpec((None,) + arr.shape[1:], lambda i: (index,) + (0,) * (arr.ndim - 1),
                        pipeline_mode=pl.Buffered(1))


def _s5_tail(yf, yb, xs, mod, layer, g, w_glu, s5_layer, w1, w2, n_lat_rows):
    rows, d = xs.shape
    row_spec = pl.BlockSpec((S5_TAIL_TILE, d), lambda i: (i, 0))
    n_lat_tiles = n_lat_rows // S5_TAIL_TILE
    return pl.pallas_call(
        _s5_tail_kernel,
        grid=(rows // S5_TAIL_TILE,),
        in_specs=[
            row_spec, row_spec, row_spec,
            pl.BlockSpec((None, None, SUBLANES, mod.shape[-1]),
                         lambda i: (layer, jnp.where(i < n_lat_tiles, 0, 1), 0, 0)),
            _resident(g), _resident(w_glu, s5_layer), _resident(w1, layer), _resident(w2, layer),
        ],
        out_specs=row_spec,
        out_shape=jax.ShapeDtypeStruct((rows, d), F32),
        compiler_params=_cparams(("arbitrary",)),
        name="s5_tail",
    )(yf, yb, xs, mod, g, w_glu, w1, w2)


def _qkv_kernel(x_ref, mod_ref, g_ref, pm_ref, w_ref, gain_ref, same_ref, cos_ref, sin_ref,
                q_ref, k_ref, v_ref, *, head_dim):
    d_model = x_ref.shape[1]
    n_batch, steps, nq = q_ref.shape
    nk = k_ref.shape[2]
    mod = mod_ref[...]
    h = _norm_mod(x_ref[...], g_ref[...], mod[:, :d_model], mod[:, d_model:]).astype(BF16)
    h = jnp.dot(pm_ref[...], h, preferred_element_type=F32).astype(BF16)
    qkv = jnp.dot(h, w_ref[...], preferred_element_type=F32)
    rows = qkv.shape[0]

    lane = lax.broadcasted_iota(jnp.int32, (rows, LANES), 1)
    low_half = lane < head_dim
    cos = jnp.tile(cos_ref[...], (n_batch, 1))
    sin = jnp.tile(sin_ref[...], (n_batch, 1))
    same_head = same_ref[...]

    def put(ref, cs, val):
        for b in range(n_batch):
            ref[b, :, cs] = val[b * steps:(b + 1) * steps]

    def norm_rope(blk, gain):
        ssum = jnp.dot((blk * blk).astype(BF16), same_head, preferred_element_type=F32)
        xn = (blk * lax.rsqrt(ssum * (1.0 / head_dim) + NORM_EPS)) * gain
        return xn * cos + pltpu.roll(xn, LANES // 2, axis=1) * sin

    for j in range(nq // LANES):
        cs = slice(j * LANES, (j + 1) * LANES)
        put(q_ref, cs, norm_rope(qkv[:, cs], gain_ref[0:1, :]).astype(BF16))
    for j in range(nk // LANES):
        cs = slice(j * LANES, (j + 1) * LANES)
        put(k_ref, cs, norm_rope(qkv[:, nq + j * LANES:nq + (j + 1) * LANES],
                                 gain_ref[1:2, :]).astype(BF16))
    for j in range(nk // LANES):
        vv = qkv[:, nq + nk + j * LANES:nq + nk + (j + 1) * LANES]
        put(v_ref, slice(2 * j * LANES, (2 * j + 1) * LANES),
            jnp.where(low_half, vv, 1.0).astype(BF16))
        put(v_ref, slice((2 * j + 1) * LANES, (2 * j + 2) * LANES),
            jnp.where(low_half, 1.0, vv).astype(BF16))


def _qkv(xs, mod, layer, g, w, gains, attn_layer, cos, sin, n_lat_tiles, head_dim):
    rows, d = xs.shape
    nq = N_HEADS * head_dim
    nk = N_KV_HEADS * LANES
    steps = ROW_TILE // SUBLANES
    t_all = rows // SUBLANES
    pm = _row_perm(ROW_TILE, to_batch_major=True)
    _, slot = _head_lane_layout(head_dim)
    same = jnp.asarray(np.equal(slot[:, None], slot[None, :]), dtype=BF16)

    def is_ctx(i):
        return jnp.where(i < n_lat_tiles, 0, 1)

    return pl.pallas_call(
        functools.partial(_qkv_kernel, head_dim=head_dim),
        grid=(rows // ROW_TILE,),
        in_specs=[
            pl.BlockSpec((ROW_TILE, d), lambda i: (i, 0)),
            pl.BlockSpec((None, None, SUBLANES, 2 * d), lambda i: (layer, is_ctx(i), 0, 0)),
            pl.BlockSpec((1, d), lambda i: (0, 0)),
            pl.BlockSpec(pm.shape, lambda i: (0, 0)),
            _resident(w, attn_layer),
            _resident(gains, attn_layer),
            pl.BlockSpec(same.shape, lambda i: (0, 0)),
            pl.BlockSpec((steps, LANES), lambda i: (i, 0)),
            pl.BlockSpec((steps, LANES), lambda i: (i, 0)),
        ],
        out_specs=[
            pl.BlockSpec((SUBLANES, steps, nq), lambda i: (0, i, 0)),
            pl.BlockSpec((SUBLANES, steps, nk), lambda i: (0, i, 0)),
            pl.BlockSpec((SUBLANES, steps, 2 * nk), lambda i: (0, i, 0)),
        ],
        out_shape=[
            jax.ShapeDtypeStruct((SUBLANES, t_all, nq), BF16),
            jax.ShapeDtypeStruct((SUBLANES, t_all, nk), BF16),
            jax.ShapeDtypeStruct((SUBLANES, t_all, 2 * nk), BF16),
        ],
        compiler_params=_cparams(("arbitrary",)),
        name="attn_qkv",
    )(xs, mod, g, pm, w, gains, same, cos, sin)


def _attn_kernel(q_ref, k_ref, v_ref, o_ref, *, head_dim):
    tq, nq = q_ref.shape
    n_heads = nq // head_dim
    q_per_kv = n_heads // (k_ref.shape[1] // LANES)
    lane = lax.broadcasted_iota(jnp.int32, (tq, LANES), 1)
    low_head = lane < head_dim
    first_slot = (lane & (head_dim // 2)) == 0

    def scores(h):
        j, hi_head = divmod(h, 2)
        kv = h // q_per_kv
        qblk = q_ref[:, j * LANES:(j + 1) * LANES].astype(F32)
        keep = jnp.logical_not(first_slot) if hi_head else first_slot
        qh = jnp.where(keep, qblk, 0.0).astype(BF16)
        k = k_ref[:, kv * LANES:(kv + 1) * LANES]
        return lax.dot_general(qh, k, (((1,), (1,)), ((), ())), preferred_element_type=F32)

    ahead = [scores(h) for h in range(min(SCORE_LOOKAHEAD, n_heads))]
    lo = None
    for h in range(n_heads):
        s = ahead.pop(0)
        if h + SCORE_LOOKAHEAD < n_heads:
            ahead.append(scores(h + SCORE_LOOKAHEAD))
        e = jnp.exp2(s - jnp.max(s, axis=-1, keepdims=True)).astype(BF16)
        vcol = (2 * (h // q_per_kv) + h % 2) * LANES
        res = jnp.dot(e, v_ref[:, vcol:vcol + LANES], preferred_element_type=F32)
        if h % 2 == 0:
            lo = res
        else:
            num = jnp.where(low_head, lo, res)
            den = pltpu.roll(jnp.where(low_head, res, lo), head_dim, axis=1)
            j = h // 2
            o_ref[:, j * LANES:(j + 1) * LANES] = (num / den).astype(BF16)


def _attention(q, k2, v2, q_start, q_len, q_tile, key_start, key_len, head_dim):
    n_batch, _, nq = q.shape
    assert q_start % q_tile == 0 and q_len % q_tile == 0 and key_start % key_len == 0
    q0 = q_start // q_tile
    k0 = key_start // key_len
    return pl.pallas_call(
        functools.partial(_attn_kernel, head_dim=head_dim),
        grid=(n_batch, q_len // q_tile),
        in_specs=[
            pl.BlockSpec((None, q_tile, nq), lambda b, i: (b, i + q0, 0)),
            pl.BlockSpec((None, key_len, k2.shape[2]), lambda b, i: (b, k0, 0)),
            pl.BlockSpec((None, key_len, v2.shape[2]), lambda b, i: (b, k0, 0)),
        ],
        out_specs=pl.BlockSpec((None, q_tile, nq), lambda b, i: (b, i, 0)),
        out_shape=jax.ShapeDtypeStruct((n_batch, q_len, nq), BF16),
        compiler_params=_cparams(("arbitrary", "arbitrary")),
        name="attention",
    )(q, k2, v2)


def _attn_tail(o_lat, o_ctx, xs, mod, layer, g, w_o, attn_layer, w1, w2, final_g=None):
    n_batch, n_lat, d = o_lat.shape
    steps = ROW_TILE // n_batch
    n_lat_tiles = n_lat // steps
    n_tiles = n_lat_tiles + (o_ctx.shape[1] // steps if o_ctx is not None else 0)
    pm = _row_perm(ROW_TILE, to_batch_major=False)
    operands = [o_lat]
    specs = [pl.BlockSpec((n_batch, steps, d), lambda i: (0, jnp.minimum(i, n_lat_tiles - 1), 0))]
    if o_ctx is not None:
        operands.append(o_ctx)
        specs.append(
            pl.BlockSpec((n_batch, steps, d), lambda i: (0, jnp.maximum(i - n_lat_tiles, 0), 0)))
    operands += [xs, mod, g, pm, w_o, w1, w2]
    specs += [
        pl.BlockSpec((ROW_TILE, d), lambda i: (i, 0)),
        pl.BlockSpec((None, None, SUBLANES, mod.shape[-1]),
                     lambda i: (layer, jnp.where(i < n_lat_tiles, 0, 1), 0, 0)),
        _resident(g), _resident(pm), _resident(w_o, attn_layer), _resident(w1, layer),
        _resident(w2, layer),
    ]
    if final_g is not None:
        assert o_ctx is None
        pmf = _row_perm(ROW_TILE, to_batch_major=True)
        operands += [final_g, pmf]
        specs += [_resident(final_g), _resident(pmf)]
        out_spec = pl.BlockSpec((n_batch, steps, d), lambda i: (0, i, 0))
        out_shape = jax.ShapeDtypeStruct((n_batch, n_lat, d), F32)
    else:
        out_spec = pl.BlockSpec((ROW_TILE, d), lambda i: (i, 0))
        out_shape = jax.ShapeDtypeStruct((n_tiles * ROW_TILE, d), F32)
    return pl.pallas_call(
        functools.partial(_attn_tail_kernel, n_lat_tiles=n_lat_tiles, has_ctx=o_ctx is not None,
                          final=final_g is not None),
        grid=(n_tiles,),
        in_specs=specs,
        out_specs=out_spec,
        out_shape=out_shape,
        compiler_params=_cparams(("arbitrary",)),
        name="attn_tail",
    )(*operands)


def _final_kernel(x_ref, g_ref, pm_ref, o_ref):
    x = x_ref[...]
    y = (x * _rms(x)) * g_ref[...]
    o_ref[...] = _permute_exact(pm_ref[...], y).reshape(o_ref.shape)


def _final_norm(xs, g, n_batch, seq):
    d = xs.shape[1]
    steps = ROW_TILE // n_batch
    pm = _row_perm(ROW_TILE, to_batch_major=True)
    return pl.pallas_call(
        _final_kernel,
        grid=(seq // steps,),
        in_specs=[
            pl.BlockSpec((ROW_TILE, d), lambda j: (j, 0)),
            pl.BlockSpec((1, d), lambda j: (0, 0)),
            pl.BlockSpec(pm.shape, lambda j: (0, 0)),
        ],
        out_specs=pl.BlockSpec((n_batch, steps, d), lambda j: (0, j, 0)),
        out_shape=jax.ShapeDtypeStruct((n_batch, seq, d), F32),
        compiler_params=_cparams(("arbitrary",)),
        name="final_norm",
    )(xs, g, pm)


def _block_diag(blocks, per_block):
    *lead, g, r, c = blocks.shape
    nb = g // per_block
    eye = jnp.eye(per_block, dtype=blocks.dtype)
    out = jnp.einsum('...ngrc,gk->...ngrkc', blocks.reshape(*lead, nb, per_block, r, c), eye)
    return out.reshape(*lead, nb, per_block * r, per_block * c)


def _s5_params(a_re, a_im, log_dt, b_re, b_im, c_re, c_im):
    per_block = S5_OUT_BLOCK // S5_GROUP
    ar = a_re.astype(F32)
    ai = a_im.astype(F32)
    dt = jnp.exp(log_dt.astype(F32))[..., None]
    mag = jnp.exp(dt * ar)
    abar_re = mag * jnp.cos(dt * ai)
    abar_im = mag * jnp.sin(dt * ai)
    den = ar * ar + ai * ai
    nr = abar_re - 1.0
    f_re = (nr * ar + abar_im * ai) / den
    f_im = (abar_im * ar - nr * ai) / den
    br = b_re.astype(F32)
    bi = b_im.astype(F32)
    bb_re = f_re[..., None] * br - f_im[..., None] * bi
    bb_im = f_re[..., None] * bi + f_im[..., None] * br
    assert S5_IN_BLOCK == S5_OUT_BLOCK
    slabs = _block_diag(jnp.stack([bb_re, bb_im, jnp.swapaxes(c_re.astype(F32), -1, -2),
                                   -jnp.swapaxes(c_im.astype(F32), -1, -2)]), per_block)
    bw = jnp.concatenate([slabs[0], slabs[1]], axis=-2)
    cw = jnp.concatenate([slabs[2], slabs[3]], axis=-2)
    lead = ar.shape[:2]
    ab = jnp.stack([abar_re.reshape(*lead, -1), abar_im.reshape(*lead, -1)], axis=2)
    ab = jnp.broadcast_to(ab[:, :, :, None, :], (*lead, 2, SUBLANES, ab.shape[-1]))
    return ab, bw.astype(BF16), cw.astype(BF16)


def _head_lane_layout(head_dim):
    quarter = head_dim // 4
    lane = np.arange(LANES)
    second = lane // (LANES // 2)
    pos = lane % (head_dim // 2)
    dim = np.where(pos < quarter, pos, 2 * quarter + (pos - quarter)) + quarter * second
    slot = (lane % (LANES // 2)) // (head_dim // 2)
    return dim, slot


def _rope_tables(n_ctx, seq, head_dim):
    quarter = head_dim // 4
    pos = jnp.arange(seq, dtype=jnp.int32)
    row = (pos // GRID_W).astype(F32)
    col = (pos % GRID_W).astype(F32)
    inv_freq = ROPE_THETA ** (-jnp.arange(quarter, dtype=F32) / quarter)
    ang = jnp.concatenate([row[:, None] * inv_freq[None, :], col[:, None] * inv_freq[None, :]],
                          axis=-1)
    ang = jnp.tile(ang, (1, LANES // ang.shape[1]))
    sign = jnp.where(jnp.arange(LANES) < LANES // 2, -1.0, 1.0)
    cos = jnp.concatenate([jnp.cos(ang), jnp.ones((n_ctx, LANES), F32)], axis=0)
    sin = jnp.concatenate([jnp.sin(ang) * sign, jnp.zeros((n_ctx, LANES), F32)], axis=0)
    return cos, sin


def _qkv_params(w_qkv, q_g, k_g, head_dim):
    nq = N_HEADS * head_dim
    nkv = N_KV_HEADS * head_dim
    per_block = LANES // head_dim
    quarter = head_dim // 4
    dim, _ = _head_lane_layout(head_dim)
    lead = w_qkv.shape[:-1]
    nl = len(lead)

    def lanes(w, n_heads, slots):
        w = w.reshape(*lead, n_heads // slots, slots, 2, 2, quarter)
        w = jnp.transpose(w, (*range(nl), nl, nl + 3, nl + 1, nl + 2, nl + 4))
        w = jnp.broadcast_to(w, (*lead, n_heads // slots, 2, per_block, 2, quarter))
        return w.reshape(*lead, -1)

    v = w_qkv[..., nq + nkv:].reshape(*lead, N_KV_HEADS, 1, head_dim)
    v = jnp.broadcast_to(v, (*lead, N_KV_HEADS, per_block, head_dim)).reshape(*lead, -1)
    w = jnp.concatenate([lanes(w_qkv[..., :nq], N_HEADS, per_block),
                         lanes(w_qkv[..., nq:nq + nkv], N_KV_HEADS, 1), v], axis=-1)
    q_scale = (head_dim ** -0.5) * LOG2_E
    gains = jnp.stack([q_g[..., dim] * q_scale, k_g[..., dim]], axis=-2)
    return w.astype(BF16), gains


def kernel(x, c, ctx, c_ctx, ada_w, ada_b, norm_mix_g, norm_ffn_g, s5_a_re, s5_a_im, s5_log_dt,
           s5_b_re, s5_b_im, s5_c_re, s5_c_im, s5_d, s5_w_glu, attn_w_qkv, attn_q_g, attn_k_g,
           attn_w_o, ffn_w1, ffn_w2, final_g):
    n_batch, seq, d = x.shape
    n_ctx = ctx.shape[1]
    depth = ada_w.shape[0]
    head_dim = d // N_HEADS
    t_all = n_ctx + seq
    assert n_batch == SUBLANES and seq % Q_TILE == 0 and seq % n_ctx == 0
    assert n_ctx % SCAN_STEPS == 0 and seq % SCAN_STEPS == 0 and d % S5_IN_BLOCK == 0
    lat_row_tiles = seq * n_batch // ROW_TILE

    xs = _to_stream(x, ctx)
    c2 = jnp.concatenate([c, jnp.broadcast_to(c_ctx[None, :], (SUBLANES, d))], axis=0)
    mod = _adaln(c2, ada_w, ada_b).reshape(depth, 2, SUBLANES, 6 * d)
    cos, sin = _rope_tables(n_ctx, seq, head_dim)

    w1 = ffn_w1.astype(BF16)
    w2 = ffn_w2.astype(BF16)
    w_glu = s5_w_glu.astype(BF16)
    w_o = attn_w_o.astype(BF16)
    s5_abar, s5_bw, s5_cw = _s5_params(s5_a_re, s5_a_im, s5_log_dt, s5_b_re, s5_b_im, s5_c_re, s5_c_im)
    w_qkv, qk_gains = _qkv_params(attn_w_qkv, attn_q_g, attn_k_g, head_dim)

    for i in range(depth):
        last = i == depth - 1
        j = i // 2
        g_mix = norm_mix_g[i].reshape(1, d)
        g_ffn = norm_ffn_g[i].reshape(1, d)
        if i % 2 == 0:
            dvec = s5_d[j].reshape(1, d)
            yf = _s5_scan(xs, mod, i, g_mix, dvec, s5_abar, s5_bw, s5_cw, j, n_ctx, reverse=False)
            yb = _s5_scan(xs, mod, i, g_mix, dvec, s5_abar, s5_bw, s5_cw, j, n_ctx, reverse=True)
            xs = _s5_tail(yf, yb, xs, mod, i, g_ffn, w_glu, j, w1, w2, seq * n_batch)
        else:
            q, k2, v2 = _qkv(xs, mod, i, g_mix, w_qkv, qk_gains, j, cos, sin, lat_row_tiles, head_dim)
            o_lat = _attention(q, k2, v2, 0, seq, Q_TILE, 0, t_all, head_dim)
            o_ctx = None if last else _attention(q, k2, v2, seq, n_ctx, n_ctx, seq, n_ctx, head_dim)
            xs = _attn_tail(o_lat, o_ctx, xs, mod, i, g_ffn, w_o, j, w1, w2,
                            final_g.reshape(1, d) if last else None)
            if last:
                return xs

    return _final_norm(xs, final_g.reshape(1, d), n_batch, seq)
```

```python
import functools

import numpy as np
import jax
import jax.numpy as jnp
from jax import lax
from jax.experimental import pallas as pl
from jax.experimental.pallas import tpu as pltpu

F32 = jnp.float32
BF16 = jnp.bfloat16

GRID_W = 64
N_HEADS = 16
N_KV_HEADS = 4
S5_GROUP = 16
ROPE_THETA = 10000.0
NORM_EPS = 1e-6
LOG2_E = 1.4426950408889634

SUBLANES = 8
LANES = 128

ROW_TILE = 256
S5_TAIL_TILE = 512
SCAN_STEPS = 32
Q_TILE = 512
SCORE_LOOKAHEAD = 2
S5_IN_BLOCK = LANES
S5_OUT_BLOCK = LANES
VMEM_LIMIT = 48 * 1024 * 1024


def _cparams(semantics):
    return pltpu.CompilerParams(dimension_semantics=semantics, vmem_limit_bytes=VMEM_LIMIT)


def _per_batch(v, rows):
    return jnp.tile(v, (rows // SUBLANES, 1))


def _rms(x):
    return lax.rsqrt(jnp.mean(x * x, axis=-1, keepdims=True) + NORM_EPS)


def _norm_mod(x, g, shift, scale):
    h = (x * _rms(x)) * g
    return h * _per_batch(1.0 + scale, x.shape[0]) + _per_batch(shift, x.shape[0])


def _row_perm(rows, to_batch_major):
    nt = rows // SUBLANES
    r = np.arange(rows)
    if to_batch_major:
        src = (r % nt) * SUBLANES + r // nt
    else:
        src = (r % SUBLANES) * nt + r // SUBLANES
    return jnp.asarray(np.equal(src[:, None], r[None, :]), dtype=BF16)


def _permute_exact(pm, x):
    hi = x.astype(BF16)
    r1 = x - hi.astype(F32)
    mid = r1.astype(BF16)
    lo = (r1 - mid.astype(F32)).astype(BF16)
    return (jnp.dot(pm, hi, preferred_element_type=F32)
            + jnp.dot(pm, mid, preferred_element_type=F32)
            + jnp.dot(pm, lo, preferred_element_type=F32))


def _to_stream_kernel(pm_ref, x_ref, ctx_ref, o_ref, *, n_lat_tiles):
    j = pl.program_id(0)
    rows = o_ref.shape[0]

    @pl.when(j < n_lat_tiles)
    def _():
        o_ref[...] = _permute_exact(pm_ref[...], x_ref[...].reshape(rows, -1))

    @pl.when(j >= n_lat_tiles)
    def _():
        o_ref[...] = _permute_exact(pm_ref[...], ctx_ref[...].reshape(rows, -1))


def _to_stream(x, ctx):
    b, l, d = x.shape
    cl = ctx.shape[1]
    steps = ROW_TILE // b
    n_lat_tiles = l // steps
    pm = _row_perm(ROW_TILE, to_batch_major=False)
    return pl.pallas_call(
        functools.partial(_to_stream_kernel, n_lat_tiles=n_lat_tiles),
        grid=((cl + l) // steps,),
        in_specs=[
            pl.BlockSpec(pm.shape, lambda j: (0, 0)),
            pl.BlockSpec((b, steps, d), lambda j: (0, jnp.minimum(j, n_lat_tiles - 1), 0)),
            pl.BlockSpec((b, steps, d), lambda j: (0, jnp.maximum(j - n_lat_tiles, 0), 0)),
        ],
        out_specs=pl.BlockSpec((ROW_TILE, d), lambda j: (j, 0)),
        out_shape=jax.ShapeDtypeStruct(((cl + l) * b, d), F32),
        compiler_params=_cparams(("arbitrary",)),
        name="to_stream",
    )(pm, x, ctx)


def _adaln_kernel(c_ref, w_ref, b_ref, o_ref):
    s = jax.nn.silu(c_ref[...]).astype(BF16)
    o_ref[...] = jnp.dot(s, w_ref[...].astype(BF16), preferred_element_type=F32) + b_ref[...]


def _adaln(c2, ada_w, ada_b):
    depth, d, n = ada_w.shape
    tn = n // 4
    return pl.pallas_call(
        _adaln_kernel,
        grid=(depth, n // tn),
        in_specs=[
            pl.BlockSpec(c2.shape, lambda i, j: (0, 0)),
            pl.BlockSpec((None, d, tn), lambda i, j: (i, 0, j)),
            pl.BlockSpec((None, 1, tn), lambda i, j: (i, 0, j)),
        ],
        out_specs=pl.BlockSpec((None, c2.shape[0], tn), lambda i, j: (i, 0, j)),
        out_shape=jax.ShapeDtypeStruct((depth, c2.shape[0], n), F32),
        compiler_params=_cparams(("arbitrary", "arbitrary")),
        name="adaln",
    )(c2, ada_w, ada_b.reshape(depth, 1, n))


def _s5_scan_kernel(x_ref, mod_ref, g_ref, d_ref, a_ref, bw_ref, cw_ref, y_ref, bu_ref, st_ref,
                    *, steps, reverse):
    d_model = x_ref.shape[1]
    n_in, st_in, ch_in = bw_ref.shape
    n_out, st_out, ch_out = cw_ref.shape
    half = st_out // 2

    @pl.when(pl.program_id(0) == 0)
    def _():
        st_ref[...] = jnp.zeros_like(st_ref)

    mod = mod_ref[...]
    u = _norm_mod(x_ref[...], g_ref[...], mod[:, :d_model], mod[:, d_model:])
    ub = u.astype(BF16)

    for j in range(n_in):
        bu_ref[:, j * st_in:(j + 1) * st_in] = lax.dot_general(
            ub[:, j * ch_in:(j + 1) * ch_in], bw_ref[j], (((1,), (1,)), ((), ())),
            preferred_element_type=F32)

    order = range(steps - 1, -1, -1) if reverse else range(steps)
    for c in range(n_out):
        re = slice(c * st_out, c * st_out + half)
        im = slice(c * st_out + half, (c + 1) * st_out)
        sc = slice(c * half, (c + 1) * half)
        ar = a_ref[0, :, sc]
        ai = a_ref[1, :, sc]
        hr = st_ref[0, :, sc]
        hi = st_ref[1, :, sc]
        for t in order:
            rows = slice(t * SUBLANES, (t + 1) * SUBLANES)
            nr = ar * hr - ai * hi + bu_ref[rows, re]
            ni = ar * hi + ai * hr + bu_ref[rows, im]
            bu_ref[rows, re] = nr
            bu_ref[rows, im] = ni
            hr, hi = nr, ni
        st_ref[0, :, sc] = hr
        st_ref[1, :, sc] = hi

        hb = bu_ref[:, c * st_out:(c + 1) * st_out].astype(BF16)
        y = jnp.dot(hb, cw_ref[c], preferred_element_type=F32)
        cs = slice(c * ch_out, (c + 1) * ch_out)
        if not reverse:
            y = y + d_ref[:, cs] * u[:, cs]
        y_ref[:, cs] = y


def _s5_scan(xs, mod, layer, g, dvec, abar, bw, cw, s5_layer, n_ctx_steps, reverse):
    rows, d = xs.shape
    tile = SCAN_STEPS * SUBLANES
    n_tiles = rows // tile
    n_ctx = n_ctx_steps // SCAN_STEPS
    n_states = abar.shape[-1]
    n_lat = n_tiles - n_ctx
    dirn = int(reverse)

    def picked(arr):
        return pl.BlockSpec((None, None) + arr.shape[2:],
                            lambda k: (s5_layer, dirn) + (0,) * (arr.ndim - 2),
                            pipeline_mode=pl.Buffered(1))

    def tile_of(k):
        if reverse:
            return n_tiles - 1 - k
        return jnp.where(k < n_ctx, n_lat + k, k - n_ctx)

    def is_ctx(k):
        return jnp.where(k < n_ctx, 1, 0)

    return pl.pallas_call(
        functools.partial(_s5_scan_kernel, steps=SCAN_STEPS, reverse=reverse),
        grid=(n_tiles,),
        in_specs=[
            pl.BlockSpec((tile, d), lambda k: (tile_of(k), 0)),
            pl.BlockSpec((None, None, SUBLANES, 2 * d), lambda k: (layer, is_ctx(k), 0, 0)),
            pl.BlockSpec((1, d), lambda k: (0, 0)),
            pl.BlockSpec((1, d), lambda k: (0, 0)),
            picked(abar), picked(bw), picked(cw),
        ],
        out_specs=pl.BlockSpec((tile, d), lambda k: (tile_of(k), 0)),
        out_shape=jax.ShapeDtypeStruct((rows, d), F32),
        scratch_shapes=[
            pltpu.VMEM((tile, 2 * n_states), F32),
            pltpu.VMEM((2, SUBLANES, n_states), F32),
        ],
        compiler_params=_cparams(("arbitrary",)),
        name="s5_scan_bwd" if reverse else "s5_scan_fwd",
    )(xs, mod, g, dvec, abar, bw, cw)


def _mlp_sublayer(x, mod, g, w1_ref, w2_ref):
    d = x.shape[1]
    h = _norm_mod(x, g, mod[:, 3 * d:4 * d], mod[:, 4 * d:5 * d]).astype(BF16)
    acc = jnp.zeros(x.shape, F32)
    for c in range(w1_ref.shape[1] // d):
        cs = slice(c * d, (c + 1) * d)
        a = jnp.dot(h, w1_ref[:, cs], preferred_element_type=F32)
        a = jnp.square(jnp.maximum(a, 0.0)).astype(BF16)
        acc = acc + jnp.dot(a, w2_ref[cs, :], preferred_element_type=F32)
    return x + _per_batch(mod[:, 5 * d:], x.shape[0]) * acc


def _s5_tail_kernel(yf_ref, yb_ref, x_ref, mod_ref, g_ref, wg_ref, w1_ref, w2_ref, o_ref):
    d = x_ref.shape[1]
    mod = mod_ref[...]
    y = yf_ref[...] + yb_ref[...]
    z = jnp.dot(jax.nn.gelu(y).astype(BF16), wg_ref[...], preferred_element_type=F32)
    glu = z[:, :d] * jax.nn.sigmoid(z[:, d:])
    x1 = x_ref[...] + _per_batch(mod[:, 2 * d:3 * d], glu.shape[0]) * glu
    o_ref[...] = _mlp_sublayer(x1, mod, g_ref[...], w1_ref, w2_ref)


def _attn_tail_kernel(*refs, n_lat_tiles, has_ctx, final):
    refs = list(refs)
    ol_ref = refs.pop(0)
    o = ol_ref[...]
    if has_ctx:
        o = jnp.where(pl.program_id(0) < n_lat_tiles, o, refs.pop(0)[...])
    x_ref, mod_ref, g_ref, pm_ref, wo_ref, w1_ref, w2_ref = refs[:7]
    y_ref = refs[-1]
    rows, d = x_ref.shape
    mod = mod_ref[...]
    o = o.reshape(rows, -1)
    o = jnp.dot(pm_ref[...], o, preferred_element_type=F32).astype(BF16)
    y = jnp.dot(o, wo_ref[...], preferred_element_type=F32)
    x1 = x_ref[...] + _per_batch(mod[:, 2 * d:3 * d], rows) * y
    x2 = _mlp_sublayer(x1, mod, g_ref[...], w1_ref, w2_ref)
    if final:
        gf_ref, pmf_ref = refs[7:9]
        x2 = (x2 * _rms(x2)) * gf_ref[...]
        y_ref[...] = _permute_exact(pmf_ref[...], x2).reshape(y_ref.shape)
    else:
        y_ref[...] = x2


def _resident(arr, index=None):
    if index is None:
        return pl.BlockSpec(arr.shape, lambda i: (0,) * arr.ndim, pipeline_mode=pl.Buffered(1))
    return pl.BlockSpec((None,) + arr.shape[1:], lambda i: (index,) + (0,) * (arr.ndim - 1),
                        pipeline_mode=pl.Buffered(1))


def _s5_tail(yf, yb, xs, mod, layer, g, w_glu, s5_layer, w1, w2, n_lat_rows):
    rows, d = xs.shape
    row_spec = pl.BlockSpec((S5_TAIL_TILE, d), lambda i: (i, 0))
    n_lat_tiles = n_lat_rows // S5_TAIL_TILE
    return pl.pallas_call(
        _s5_tail_kernel,
        grid=(rows // S5_TAIL_TILE,),
        in_specs=[
            row_spec, row_spec, row_spec,
            pl.BlockSpec((None, None, SUBLANES, mod.shape[-1]),
                         lambda i: (layer, jnp.where(i < n_lat_tiles, 0, 1), 0, 0)),
            _resident(g), _resident(w_glu, s5_layer), _resident(w1, layer), _resident(w2, layer),
        ],
        out_specs=row_spec,
        out_shape=jax.ShapeDtypeStruct((rows, d), F32),
        compiler_params=_cparams(("arbitrary",)),
        name="s5_tail",
    )(yf, yb, xs, mod, g, w_glu, w1, w2)


def _qkv_kernel(x_ref, mod_ref, g_ref, pm_ref, w_ref, gain_ref, same_ref, cos_ref, sin_ref,
                q_ref, k_ref, v_ref, *, head_dim):
    d_model = x_ref.shape[1]
    n_batch, steps, nq = q_ref.shape
    nk = k_ref.shape[2]
    mod = mod_ref[...]
    h = _norm_mod(x_ref[...], g_ref[...], mod[:, :d_model], mod[:, d_model:]).astype(BF16)
    h = jnp.dot(pm_ref[...], h, preferred_element_type=F32).astype(BF16)
    qkv = jnp.dot(h, w_ref[...], preferred_element_type=F32)
    rows = qkv.shape[0]

    lane = lax.broadcasted_iota(jnp.int32, (rows, LANES), 1)
    low_half = lane < head_dim
    cos = jnp.tile(cos_ref[...], (n_batch, 1))
    sin = jnp.tile(sin_ref[...], (n_batch, 1))
    same_head = same_ref[...]

    def put(ref, cs, val):
        for b in range(n_batch):
            ref[b, :, cs] = val[b * steps:(b + 1) * steps]

    def norm_rope(blk, gain):
        ssum = jnp.dot((blk * blk).astype(BF16), same_head, preferred_element_type=F32)
        xn = (blk * lax.rsqrt(ssum * (1.0 / head_dim) + NORM_EPS)) * gain
        return xn * cos + pltpu.roll(xn, LANES // 2, axis=1) * sin

    for j in range(nq // LANES):
        cs = slice(j * LANES, (j + 1) * LANES)
        put(q_ref, cs, norm_rope(qkv[:, cs], gain_ref[0:1, :]).astype(BF16))
    for j in range(nk // LANES):
        cs = slice(j * LANES, (j + 1) * LANES)
        put(k_ref, cs, norm_rope(qkv[:, nq + j * LANES:nq + (j + 1) * LANES],
                                 gain_ref[1:2, :]).astype(BF16))
    for j in range(nk // LANES):
        vv = qkv[:, nq + nk + j * LANES:nq + nk + (j + 1) * LANES]
        put(v_ref, slice(2 * j * LANES, (2 * j + 1) * LANES),
            jnp.where(low_half, vv, 1.0).astype(BF16))
        put(v_ref, slice((2 * j + 1) * LANES, (2 * j + 2) * LANES),
            jnp.where(low_half, 1.0, vv).astype(BF16))


def _qkv(xs, mod, layer, g, w, gains, attn_layer, cos, sin, n_lat_tiles, head_dim):
    rows, d = xs.shape
    nq = N_HEADS * head_dim
    nk = N_KV_HEADS * LANES
    steps = ROW_TILE // SUBLANES
    t_all = rows // SUBLANES
    pm = _row_perm(ROW_TILE, to_batch_major=True)
    _, slot = _head_lane_layout(head_dim)
    same = jnp.asarray(np.equal(slot[:, None], slot[None, :]), dtype=BF16)

    def is_ctx(i):
        return jnp.where(i < n_lat_tiles, 0, 1)

    return pl.pallas_call(
        functools.partial(_qkv_kernel, head_dim=head_dim),
        grid=(rows // ROW_TILE,),
        in_specs=[
            pl.BlockSpec((ROW_TILE, d), lambda i: (i, 0)),
            pl.BlockSpec((None, None, SUBLANES, 2 * d), lambda i: (layer, is_ctx(i), 0, 0)),
            pl.BlockSpec((1, d), lambda i: (0, 0)),
            pl.BlockSpec(pm.shape, lambda i: (0, 0)),
            _resident(w, attn_layer),
            _resident(gains, attn_layer),
            pl.BlockSpec(same.shape, lambda i: (0, 0)),
            pl.BlockSpec((steps, LANES), lambda i: (i, 0)),
            pl.BlockSpec((steps, LANES), lambda i: (i, 0)),
        ],
        out_specs=[
            pl.BlockSpec((SUBLANES, steps, nq), lambda i: (0, i, 0)),
            pl.BlockSpec((SUBLANES, steps, nk), lambda i: (0, i, 0)),
            pl.BlockSpec((SUBLANES, steps, 2 * nk), lambda i: (0, i, 0)),
        ],
        out_shape=[
            jax.ShapeDtypeStruct((SUBLANES, t_all, nq), BF16),
            jax.ShapeDtypeStruct((SUBLANES, t_all, nk), BF16),
            jax.ShapeDtypeStruct((SUBLANES, t_all, 2 * nk), BF16),
        ],
        compiler_params=_cparams(("arbitrary",)),
        name="attn_qkv",
    )(xs, mod, g, pm, w, gains, same, cos, sin)


def _attn_kernel(q_ref, k_ref, v_ref, o_ref, *, head_dim):
    tq, nq = q_ref.shape
    n_heads = nq // head_dim
    q_per_kv = n_heads // (k_ref.shape[1] // LANES)
    lane = lax.broadcasted_iota(jnp.int32, (tq, LANES), 1)
    low_head = lane < head_dim
    first_slot = (lane & (head_dim // 2)) == 0

    def scores(h):
        j, hi_head = divmod(h, 2)
        kv = h // q_per_kv
        qblk = q_ref[:, j * LANES:(j + 1) * LANES].astype(F32)
        keep = jnp.logical_not(first_slot) if hi_head else first_slot
        qh = jnp.where(keep, qblk, 0.0).astype(BF16)
        k = k_ref[:, kv * LANES:(kv + 1) * LANES]
        return lax.dot_general(qh, k, (((1,), (1,)), ((), ())), preferred_element_type=F32)

    ahead = [scores(h) for h in range(min(SCORE_LOOKAHEAD, n_heads))]
    lo = None
    for h in range(n_heads):
        s = ahead.pop(0)
        if h + SCORE_LOOKAHEAD < n_heads:
            ahead.append(scores(h + SCORE_LOOKAHEAD))
        e = jnp.exp2(s - jnp.max(s, axis=-1, keepdims=True)).astype(BF16)
        vcol = (2 * (h // q_per_kv) + h % 2) * LANES
        res = jnp.dot(e, v_ref[:, vcol:vcol + LANES], preferred_element_type=F32)
        if h % 2 == 0:
            lo = res
        else:
            num = jnp.where(low_head, lo, res)
            den = pltpu.roll(jnp.where(low_head, res, lo), head_dim, axis=1)
            j = h // 2
            o_ref[:, j * LANES:(j + 1) * LANES] = (num / den).astype(BF16)


def _attention(q, k2, v2, q_start, q_len, q_tile, key_start, key_len, head_dim):
    n_batch, _, nq = q.shape
    assert q_start % q_tile == 0 and q_len % q_tile == 0 and key_start % key_len == 0
    q0 = q_start // q_tile
    k0 = key_start // key_len
    return pl.pallas_call(
        functools.partial(_attn_kernel, head_dim=head_dim),
        grid=(n_batch, q_len // q_tile),
        in_specs=[
            pl.BlockSpec((None, q_tile, nq), lambda b, i: (b, i + q0, 0)),
            pl.BlockSpec((None, key_len, k2.shape[2]), lambda b, i: (b, k0, 0)),
            pl.BlockSpec((None, key_len, v2.shape[2]), lambda b, i: (b, k0, 0)),
        ],
        out_specs=pl.BlockSpec((None, q_tile, nq), lambda b, i: (b, i, 0)),
        out_shape=jax.ShapeDtypeStruct((n_batch, q_len, nq), BF16),
        compiler_params=_cparams(("arbitrary", "arbitrary")),
        name="attention",
    )(q, k2, v2)


def _attn_tail(o_lat, o_ctx, xs, mod, layer, g, w_o, attn_layer, w1, w2, final_g=None):
    n_batch, n_lat, d = o_lat.shape
    steps = ROW_TILE // n_batch
    n_lat_tiles = n_lat // steps
    n_tiles = n_lat_tiles + (o_ctx.shape[1] // steps if o_ctx is not None else 0)
    pm = _row_perm(ROW_TILE, to_batch_major=False)
    operands = [o_lat]
    specs = [pl.BlockSpec((n_batch, steps, d), lambda i: (0, jnp.minimum(i, n_lat_tiles - 1), 0))]
    if o_ctx is not None:
        operands.append(o_ctx)
        specs.append(
            pl.BlockSpec((n_batch, steps, d), lambda i: (0, jnp.maximum(i - n_lat_tiles, 0), 0)))
    operands += [xs, mod, g, pm, w_o, w1, w2]
    specs += [
        pl.BlockSpec((ROW_TILE, d), lambda i: (i, 0)),
        pl.BlockSpec((None, None, SUBLANES, mod.shape[-1]),
                     lambda i: (layer, jnp.where(i < n_lat_tiles, 0, 1), 0, 0)),
        _resident(g), _resident(pm), _resident(w_o, attn_layer), _resident(w1, layer),
        _resident(w2, layer),
    ]
    if final_g is not None:
        assert o_ctx is None
        pmf = _row_perm(ROW_TILE, to_batch_major=True)
        operands += [final_g, pmf]
        specs += [_resident(final_g), _resident(pmf)]
        out_spec = pl.BlockSpec((n_batch, steps, d), lambda i: (0, i, 0))
        out_shape = jax.ShapeDtypeStruct((n_batch, n_lat, d), F32)
    else:
        out_spec = pl.BlockSpec((ROW_TILE, d), lambda i: (i, 0))
        out_shape = jax.ShapeDtypeStruct((n_tiles * ROW_TILE, d), F32)
    return pl.pallas_call(
        functools.partial(_attn_tail_kernel, n_lat_tiles=n_lat_tiles, has_ctx=o_ctx is not None,
                          final=final_g is not None),
        grid=(n_tiles,),
        in_specs=specs,
        out_specs=out_spec,
        out_shape=out_shape,
        compiler_params=_cparams(("arbitrary",)),
        name="attn_tail",
    )(*operands)


def _final_kernel(x_ref, g_ref, pm_ref, o_ref):
    x = x_ref[...]
    y = (x * _rms(x)) * g_ref[...]
    o_ref[...] = _permute_exact(pm_ref[...], y).reshape(o_ref.shape)


def _final_norm(xs, g, n_batch, seq):
    d = xs.shape[1]
    steps = ROW_TILE // n_batch
    pm = _row_perm(ROW_TILE, to_batch_major=True)
    return pl.pallas_call(
        _final_kernel,
        grid=(seq // steps,),
        in_specs=[
            pl.BlockSpec((ROW_TILE, d), lambda j: (j, 0)),
            pl.BlockSpec((1, d), lambda j: (0, 0)),
            pl.BlockSpec(pm.shape, lambda j: (0, 0)),
        ],
        out_specs=pl.BlockSpec((n_batch, steps, d), lambda j: (0, j, 0)),
        out_shape=jax.ShapeDtypeStruct((n_batch, seq, d), F32),
        compiler_params=_cparams(("arbitrary",)),
        name="final_norm",
    )(xs, g, pm)


def _block_diag(blocks, per_block):
    *lead, g, r, c = blocks.shape
    nb = g // per_block
    eye = jnp.eye(per_block, dtype=blocks.dtype)
    out = jnp.einsum('...ngrc,gk->...ngrkc', blocks.reshape(*lead, nb, per_block, r, c), eye)
    return out.reshape(*lead, nb, per_block * r, per_block * c)


def _s5_params(a_re, a_im, log_dt, b_re, b_im, c_re, c_im):
    per_block = S5_OUT_BLOCK // S5_GROUP
    ar = a_re.astype(F32)
    ai = a_im.astype(F32)
    dt = jnp.exp(log_dt.astype(F32))[..., None]
    mag = jnp.exp(dt * ar)
    abar_re = mag * jnp.cos(dt * ai)
    abar_im = mag * jnp.sin(dt * ai)
    den = ar * ar + ai * ai
    nr = abar_re - 1.0
    f_re = (nr * ar + abar_im * ai) / den
    f_im = (abar_im * ar - nr * ai) / den
    br = b_re.astype(F32)
    bi = b_im.astype(F32)
    bb_re = f_re[..., None] * br - f_im[..., None] * bi
    bb_im = f_re[..., None] * bi + f_im[..., None] * br
    assert S5_IN_BLOCK == S5_OUT_BLOCK
    bw = jnp.concatenate([_block_diag(bb_re, per_block), _block_diag(bb_im, per_block)], axis=-2)
    cw = jnp.concatenate([_block_diag(jnp.swapaxes(c_re.astype(F32), -1, -2), per_block),
                          _block_diag(-jnp.swapaxes(c_im.astype(F32), -1, -2), per_block)], axis=-2)
    lead = ar.shape[:2]
    ab = jnp.stack([abar_re.reshape(*lead, -1), abar_im.reshape(*lead, -1)], axis=2)
    ab = jnp.broadcast_to(ab[:, :, :, None, :], (*lead, 2, SUBLANES, ab.shape[-1]))
    return ab, bw.astype(BF16), cw.astype(BF16)


def _head_lane_layout(head_dim):
    quarter = head_dim // 4
    lane = np.arange(LANES)
    second = lane // (LANES // 2)
    pos = lane % (head_dim // 2)
    dim = np.where(pos < quarter, pos, 2 * quarter + (pos - quarter)) + quarter * second
    slot = (lane % (LANES // 2)) // (head_dim // 2)
    return dim, slot


def _rope_tables(n_ctx, seq, head_dim):
    quarter = head_dim // 4
    pos = jnp.arange(seq, dtype=jnp.int32)
    row = (pos // GRID_W).astype(F32)
    col = (pos % GRID_W).astype(F32)
    inv_freq = ROPE_THETA ** (-jnp.arange(quarter, dtype=F32) / quarter)
    ang = jnp.concatenate([row[:, None] * inv_freq[None, :], col[:, None] * inv_freq[None, :]],
                          axis=-1)
    ang = jnp.tile(ang, (1, LANES // ang.shape[1]))
    sign = jnp.where(jnp.arange(LANES) < LANES // 2, -1.0, 1.0)
    cos = jnp.concatenate([jnp.cos(ang), jnp.ones((n_ctx, LANES), F32)], axis=0)
    sin = jnp.concatenate([jnp.sin(ang) * sign, jnp.zeros((n_ctx, LANES), F32)], axis=0)
    return cos, sin


def _qkv_params(w_qkv, q_g, k_g, head_dim):
    nq = N_HEADS * head_dim
    nkv = N_KV_HEADS * head_dim
    per_block = LANES // head_dim
    dim, slot = _head_lane_layout(head_dim)
    q_cols = np.concatenate([(per_block * j + slot) * head_dim + dim
                             for j in range(N_HEADS // per_block)])
    k_cols = np.concatenate([nq + h * head_dim + dim for h in range(N_KV_HEADS)])
    v_cols = np.concatenate([nq + nkv + h * head_dim + np.arange(LANES) % head_dim
                             for h in range(N_KV_HEADS)])
    w = w_qkv[..., np.concatenate([q_cols, k_cols, v_cols])]
    q_scale = (head_dim ** -0.5) * LOG2_E
    gains = jnp.stack([q_g[..., dim] * q_scale, k_g[..., dim]], axis=-2)
    return w.astype(BF16), gains


def kernel(x, c, ctx, c_ctx, ada_w, ada_b, norm_mix_g, norm_ffn_g, s5_a_re, s5_a_im, s5_log_dt,
           s5_b_re, s5_b_im, s5_c_re, s5_c_im, s5_d, s5_w_glu, attn_w_qkv, attn_q_g, attn_k_g,
           attn_w_o, ffn_w1, ffn_w2, final_g):
    n_batch, seq, d = x.shape
    n_ctx = ctx.shape[1]
    depth = ada_w.shape[0]
    head_dim = d // N_HEADS
    t_all = n_ctx + seq
    assert n_batch == SUBLANES and seq % Q_TILE == 0 and seq % n_ctx == 0
    assert n_ctx % SCAN_STEPS == 0 and seq % SCAN_STEPS == 0 and d % S5_IN_BLOCK == 0
    lat_row_tiles = seq * n_batch // ROW_TILE

    xs = _to_stream(x, ctx)
    c2 = jnp.concatenate([c, jnp.broadcast_to(c_ctx[None, :], (SUBLANES, d))], axis=0)
    mod = _adaln(c2, ada_w, ada_b).reshape(depth, 2, SUBLANES, 6 * d)
    cos, sin = _rope_tables(n_ctx, seq, head_dim)

    w1 = ffn_w1.astype(BF16)
    w2 = ffn_w2.astype(BF16)
    w_glu = s5_w_glu.astype(BF16)
    w_o = attn_w_o.astype(BF16)
    s5_abar, s5_bw, s5_cw = _s5_params(s5_a_re, s5_a_im, s5_log_dt, s5_b_re, s5_b_im, s5_c_re, s5_c_im)
    w_qkv, qk_gains = _qkv_params(attn_w_qkv, attn_q_g, attn_k_g, head_dim)

    for i in range(depth):
        last = i == depth - 1
        j = i // 2
        g_mix = norm_mix_g[i].reshape(1, d)
        g_ffn = norm_ffn_g[i].reshape(1, d)
        if i % 2 == 0:
            dvec = s5_d[j].reshape(1, d)
            yf = _s5_scan(xs, mod, i, g_mix, dvec, s5_abar, s5_bw, s5_cw, j, n_ctx, reverse=False)
            yb = _s5_scan(xs, mod, i, g_mix, dvec, s5_abar, s5_bw, s5_cw, j, n_ctx, reverse=True)
            xs = _s5_tail(yf, yb, xs, mod, i, g_ffn, w_glu, j, w1, w2, seq * n_batch)
        else:
            q, k2, v2 = _qkv(xs, mod, i, g_mix, w_qkv, qk_gains, j, cos, sin, lat_row_tiles, head_dim)
            o_lat = _attention(q, k2, v2, 0, seq, Q_TILE, 0, t_all, head_dim)
            o_ctx = None if last else _attention(q, k2, v2, seq, n_ctx, n_ctx, seq, n_ctx, head_dim)
            xs = _attn_tail(o_lat, o_ctx, xs, mod, i, g_ffn, w_o, j, w1, w2,
                            final_g.reshape(1, d) if last else None)
            if last:
                return xs

    return _final_norm(xs, final_g.reshape(1, d), n_batch, seq)
```

```python
import functools

import numpy as np
import jax
import jax.numpy as jnp
from jax import lax
from jax.experimental import pallas as pl
from jax.experimental.pallas import tpu as pltpu

F32 = jnp.float32
BF16 = jnp.bfloat16

GRID_W = 64
N_HEADS = 16
N_KV_HEADS = 4
S5_GROUP = 16
ROPE_THETA = 10000.0
NORM_EPS = 1e-6
LOG2_E = 1.4426950408889634

SUBLANES = 8
LANES = 128

ROW_TILE = 256
S5_TAIL_TILE = 512
SCAN_STEPS = 32
Q_TILE = 512
SCORE_LOOKAHEAD = 2
S5_IN_BLOCK = LANES
S5_OUT_BLOCK = LANES
VMEM_LIMIT = 48 * 1024 * 1024


def _cparams(semantics):
    return pltpu.CompilerParams(dimension_semantics=semantics, vmem_limit_bytes=VMEM_LIMIT)


def _per_batch(v, rows):
    return jnp.tile(v, (rows // SUBLANES, 1))


def _rms(x):
    return lax.rsqrt(jnp.mean(x * x, axis=-1, keepdims=True) + NORM_EPS)


def _norm_mod(x, g, shift, scale):
    h = (x * _rms(x)) * g
    return h * _per_batch(1.0 + scale, x.shape[0]) + _per_batch(shift, x.shape[0])


def _row_perm(rows, to_batch_major):
    nt = rows // SUBLANES
    r = np.arange(rows)
    if to_batch_major:
        src = (r % nt) * SUBLANES + r // nt
    else:
        src = (r % SUBLANES) * nt + r // SUBLANES
    return jnp.asarray(np.equal(src[:, None], r[None, :]), dtype=BF16)


def _permute_exact(pm, x):
    hi = x.astype(BF16)
    r1 = x - hi.astype(F32)
    mid = r1.astype(BF16)
    lo = (r1 - mid.astype(F32)).astype(BF16)
    return (jnp.dot(pm, hi, preferred_element_type=F32)
            + jnp.dot(pm, mid, preferred_element_type=F32)
            + jnp.dot(pm, lo, preferred_element_type=F32))


def _to_stream_kernel(pm_ref, x_ref, ctx_ref, o_ref, *, n_lat_tiles):
    j = pl.program_id(0)
    rows = o_ref.shape[0]

    @pl.when(j < n_lat_tiles)
    def _():
        o_ref[...] = _permute_exact(pm_ref[...], x_ref[...].reshape(rows, -1))

    @pl.when(j >= n_lat_tiles)
    def _():
        o_ref[...] = _permute_exact(pm_ref[...], ctx_ref[...].reshape(rows, -1))


def _to_stream(x, ctx):
    b, l, d = x.shape
    cl = ctx.shape[1]
    steps = ROW_TILE // b
    n_lat_tiles = l // steps
    pm = _row_perm(ROW_TILE, to_batch_major=False)
    return pl.pallas_call(
        functools.partial(_to_stream_kernel, n_lat_tiles=n_lat_tiles),
        grid=((cl + l) // steps,),
        in_specs=[
            pl.BlockSpec(pm.shape, lambda j: (0, 0)),
            pl.BlockSpec((b, steps, d), lambda j: (0, jnp.minimum(j, n_lat_tiles - 1), 0)),
            pl.BlockSpec((b, steps, d), lambda j: (0, jnp.maximum(j - n_lat_tiles, 0), 0)),
        ],
        out_specs=pl.BlockSpec((ROW_TILE, d), lambda j: (j, 0)),
        out_shape=jax.ShapeDtypeStruct(((cl + l) * b, d), F32),
        compiler_params=_cparams(("arbitrary",)),
        name="to_stream",
    )(pm, x, ctx)


def _adaln_kernel(c_ref, w_ref, b_ref, o_ref):
    s = jax.nn.silu(c_ref[...]).astype(BF16)
    o_ref[...] = jnp.dot(s, w_ref[...].astype(BF16), preferred_element_type=F32) + b_ref[...]


def _adaln(c2, ada_w, ada_b):
    depth, d, n = ada_w.shape
    tn = n // 4
    return pl.pallas_call(
        _adaln_kernel,
        grid=(depth, n // tn),
        in_specs=[
            pl.BlockSpec(c2.shape, lambda i, j: (0, 0)),
            pl.BlockSpec((None, d, tn), lambda i, j: (i, 0, j)),
            pl.BlockSpec((None, 1, tn), lambda i, j: (i, 0, j)),
        ],
        out_specs=pl.BlockSpec((None, c2.shape[0], tn), lambda i, j: (i, 0, j)),
        out_shape=jax.ShapeDtypeStruct((depth, c2.shape[0], n), F32),
        compiler_params=_cparams(("arbitrary", "arbitrary")),
        name="adaln",
    )(c2, ada_w, ada_b.reshape(depth, 1, n))


def _s5_scan_kernel(x_ref, mod_ref, g_ref, d_ref, a_ref, bw_ref, cw_ref, y_ref, bu_ref, st_ref,
                    *, steps, reverse):
    d_model = x_ref.shape[1]
    n_in, st_in, ch_in = bw_ref.shape
    n_out, st_out, ch_out = cw_ref.shape
    half = st_out // 2

    @pl.when(pl.program_id(0) == 0)
    def _():
        st_ref[...] = jnp.zeros_like(st_ref)

    mod = mod_ref[...]
    u = _norm_mod(x_ref[...], g_ref[...], mod[:, :d_model], mod[:, d_model:])
    ub = u.astype(BF16)

    for j in range(n_in):
        bu_ref[:, j * st_in:(j + 1) * st_in] = lax.dot_general(
            ub[:, j * ch_in:(j + 1) * ch_in], bw_ref[j], (((1,), (1,)), ((), ())),
            preferred_element_type=F32)

    order = range(steps - 1, -1, -1) if reverse else range(steps)
    for c in range(n_out):
        re = slice(c * st_out, c * st_out + half)
        im = slice(c * st_out + half, (c + 1) * st_out)
        sc = slice(c * half, (c + 1) * half)
        ar = a_ref[0, :, sc]
        ai = a_ref[1, :, sc]
        hr = st_ref[0, :, sc]
        hi = st_ref[1, :, sc]
        for t in order:
            rows = slice(t * SUBLANES, (t + 1) * SUBLANES)
            nr = ar * hr - ai * hi + bu_ref[rows, re]
            ni = ar * hi + ai * hr + bu_ref[rows, im]
            bu_ref[rows, re] = nr
            bu_ref[rows, im] = ni
            hr, hi = nr, ni
        st_ref[0, :, sc] = hr
        st_ref[1, :, sc] = hi

        hb = bu_ref[:, c * st_out:(c + 1) * st_out].astype(BF16)
        y = jnp.dot(hb, cw_ref[c], preferred_element_type=F32)
        cs = slice(c * ch_out, (c + 1) * ch_out)
        if not reverse:
            y = y + d_ref[:, cs] * u[:, cs]
        y_ref[:, cs] = y


def _s5_scan(xs, mod, layer, g, dvec, abar, bw, cw, s5_layer, n_ctx_steps, reverse):
    rows, d = xs.shape
    tile = SCAN_STEPS * SUBLANES
    n_tiles = rows // tile
    n_ctx = n_ctx_steps // SCAN_STEPS
    n_states = abar.shape[-1]
    n_lat = n_tiles - n_ctx
    dirn = int(reverse)

    def picked(arr):
        return pl.BlockSpec((None, None) + arr.shape[2:],
                            lambda k: (s5_layer, dirn) + (0,) * (arr.ndim - 2),
                            pipeline_mode=pl.Buffered(1))

    def tile_of(k):
        if reverse:
            return n_tiles - 1 - k
        return jnp.where(k < n_ctx, n_lat + k, k - n_ctx)

    def is_ctx(k):
        return jnp.where(k < n_ctx, 1, 0)

    return pl.pallas_call(
        functools.partial(_s5_scan_kernel, steps=SCAN_STEPS, reverse=reverse),
        grid=(n_tiles,),
        in_specs=[
            pl.BlockSpec((tile, d), lambda k: (tile_of(k), 0)),
            pl.BlockSpec((None, None, SUBLANES, 2 * d), lambda k: (layer, is_ctx(k), 0, 0)),
            pl.BlockSpec((1, d), lambda k: (0, 0)),
            pl.BlockSpec((1, d), lambda k: (0, 0)),
            picked(abar), picked(bw), picked(cw),
        ],
        out_specs=pl.BlockSpec((tile, d), lambda k: (tile_of(k), 0)),
        out_shape=jax.ShapeDtypeStruct((rows, d), F32),
        scratch_shapes=[
            pltpu.VMEM((tile, 2 * n_states), F32),
            pltpu.VMEM((2, SUBLANES, n_states), F32),
        ],
        compiler_params=_cparams(("arbitrary",)),
        name="s5_scan_bwd" if reverse else "s5_scan_fwd",
    )(xs, mod, g, dvec, abar, bw, cw)


def _mlp_sublayer(x, mod, g, w1_ref, w2_ref):
    d = x.shape[1]
    h = _norm_mod(x, g, mod[:, 3 * d:4 * d], mod[:, 4 * d:5 * d]).astype(BF16)
    acc = jnp.zeros(x.shape, F32)
    for c in range(w1_ref.shape[1] // d):
        cs = slice(c * d, (c + 1) * d)
        a = jnp.dot(h, w1_ref[:, cs], preferred_element_type=F32)
        a = jnp.square(jnp.maximum(a, 0.0)).astype(BF16)
        acc = acc + jnp.dot(a, w2_ref[cs, :], preferred_element_type=F32)
    return x + _per_batch(mod[:, 5 * d:], x.shape[0]) * acc


def _s5_tail_kernel(yf_ref, yb_ref, x_ref, mod_ref, g_ref, wg_ref, w1_ref, w2_ref, o_ref):
    d = x_ref.shape[1]
    mod = mod_ref[...]
    y = yf_ref[...] + yb_ref[...]
    z = jnp.dot(jax.nn.gelu(y).astype(BF16), wg_ref[...], preferred_element_type=F32)
    glu = z[:, :d] * jax.nn.sigmoid(z[:, d:])
    x1 = x_ref[...] + _per_batch(mod[:, 2 * d:3 * d], glu.shape[0]) * glu
    o_ref[...] = _mlp_sublayer(x1, mod, g_ref[...], w1_ref, w2_ref)


def _attn_tail_kernel(*refs, n_lat_tiles, has_ctx, final):
    refs = list(refs)
    ol_ref = refs.pop(0)
    o = ol_ref[...]
    if has_ctx:
        o = jnp.where(pl.program_id(0) < n_lat_tiles, o, refs.pop(0)[...])
    x_ref, mod_ref, g_ref, pm_ref, wo_ref, w1_ref, w2_ref = refs[:7]
    y_ref = refs[-1]
    rows, d = x_ref.shape
    mod = mod_ref[...]
    o = o.reshape(rows, -1)
    o = jnp.dot(pm_ref[...], o, preferred_element_type=F32).astype(BF16)
    y = jnp.dot(o, wo_ref[...], preferred_element_type=F32)
    x1 = x_ref[...] + _per_batch(mod[:, 2 * d:3 * d], rows) * y
    x2 = _mlp_sublayer(x1, mod, g_ref[...], w1_ref, w2_ref)
    if final:
        gf_ref, pmf_ref = refs[7:9]
        x2 = (x2 * _rms(x2)) * gf_ref[...]
        y_ref[...] = _permute_exact(pmf_ref[...], x2).reshape(y_ref.shape)
    else:
        y_ref[...] = x2


def _resident(arr, index=None):
    if index is None:
        return pl.BlockSpec(arr.shape, lambda i: (0,) * arr.ndim, pipeline_mode=pl.Buffered(1))
    return pl.BlockSpec((None,) + arr.shape[1:], lambda i: (index,) + (0,) * (arr.ndim - 1),
                        pipeline_mode=pl.Buffered(1))


def _s5_tail(yf, yb, xs, mod, layer, g, w_glu, s5_layer, w1, w2, n_lat_rows):
    rows, d = xs.shape
    row_spec = pl.BlockSpec((S5_TAIL_TILE, d), lambda i: (i, 0))
    n_lat_tiles = n_lat_rows // S5_TAIL_TILE
    return pl.pallas_call(
        _s5_tail_kernel,
        grid=(rows // S5_TAIL_TILE,),
        in_specs=[
            row_spec, row_spec, row_spec,
            pl.BlockSpec((None, None, SUBLANES, mod.shape[-1]),
                         lambda i: (layer, jnp.where(i < n_lat_tiles, 0, 1), 0, 0)),
            _resident(g), _resident(w_glu, s5_layer), _resident(w1, layer), _resident(w2, layer),
        ],
        out_specs=row_spec,
        out_shape=jax.ShapeDtypeStruct((rows, d), F32),
        compiler_params=_cparams(("arbitrary",)),
        name="s5_tail",
    )(yf, yb, xs, mod, g, w_glu, w1, w2)


def _qkv_kernel(x_ref, mod_ref, g_ref, pm_ref, w_ref, gain_ref, same_ref, cos_ref, sin_ref,
                q_ref, k_ref, v_ref, *, head_dim):
    d_model = x_ref.shape[1]
    n_batch, steps, nq = q_ref.shape
    nk = k_ref.shape[2]
    mod = mod_ref[...]
    h = _norm_mod(x_ref[...], g_ref[...], mod[:, :d_model], mod[:, d_model:]).astype(BF16)
    h = jnp.dot(pm_ref[...], h, preferred_element_type=F32).astype(BF16)
    qkv = jnp.dot(h, w_ref[...], preferred_element_type=F32)
    rows = qkv.shape[0]

    lane = lax.broadcasted_iota(jnp.int32, (rows, LANES), 1)
    low_half = lane < head_dim
    cos = jnp.tile(cos_ref[...], (n_batch, 1))
    sin = jnp.tile(sin_ref[...], (n_batch, 1))
    same_head = same_ref[...]

    def put(ref, cs, val):
        for b in range(n_batch):
            ref[b, :, cs] = val[b * steps:(b + 1) * steps]

    def norm_rope(blk, gain):
        ssum = jnp.dot((blk * blk).astype(BF16), same_head, preferred_element_type=F32)
        xn = (blk * lax.rsqrt(ssum * (1.0 / head_dim) + NORM_EPS)) * gain
        return xn * cos + pltpu.roll(xn, LANES // 2, axis=1) * sin

    for j in range(nq // LANES):
        cs = slice(j * LANES, (j + 1) * LANES)
        put(q_ref, cs, norm_rope(qkv[:, cs], gain_ref[0:1, :]).astype(BF16))
    for j in range(nk // LANES):
        cs = slice(j * LANES, (j + 1) * LANES)
        put(k_ref, cs, norm_rope(qkv[:, nq + j * LANES:nq + (j + 1) * LANES],
                                 gain_ref[1:2, :]).astype(BF16))
    for j in range(nk // LANES):
        vv = qkv[:, nq + nk + j * LANES:nq + nk + (j + 1) * LANES]
        put(v_ref, slice(2 * j * LANES, (2 * j + 1) * LANES),
            jnp.where(low_half, vv, 1.0).astype(BF16))
        put(v_ref, slice((2 * j + 1) * LANES, (2 * j + 2) * LANES),
            jnp.where(low_half, 1.0, vv).astype(BF16))


def _qkv(xs, mod, layer, g, w, gains, attn_layer, cos, sin, n_lat_tiles, head_dim):
    rows, d = xs.shape
    nq = N_HEADS * head_dim
    nk = N_KV_HEADS * LANES
    steps = ROW_TILE // SUBLANES
    t_all = rows // SUBLANES
    pm = _row_perm(ROW_TILE, to_batch_major=True)
    _, slot = _head_lane_layout(head_dim)
    same = jnp.asarray(np.equal(slot[:, None], slot[None, :]), dtype=BF16)

    def is_ctx(i):
        return jnp.where(i < n_lat_tiles, 0, 1)

    return pl.pallas_call(
        functools.partial(_qkv_kernel, head_dim=head_dim),
        grid=(rows // ROW_TILE,),
        in_specs=[
            pl.BlockSpec((ROW_TILE, d), lambda i: (i, 0)),
            pl.BlockSpec((None, None, SUBLANES, 2 * d), lambda i: (layer, is_ctx(i), 0, 0)),
            pl.BlockSpec((1, d), lambda i: (0, 0)),
            pl.BlockSpec(pm.shape, lambda i: (0, 0)),
            _resident(w, attn_layer),
            _resident(gains, attn_layer),
            pl.BlockSpec(same.shape, lambda i: (0, 0)),
            pl.BlockSpec((steps, LANES), lambda i: (i, 0)),
            pl.BlockSpec((steps, LANES), lambda i: (i, 0)),
        ],
        out_specs=[
            pl.BlockSpec((SUBLANES, steps, nq), lambda i: (0, i, 0)),
            pl.BlockSpec((SUBLANES, steps, nk), lambda i: (0, i, 0)),
            pl.BlockSpec((SUBLANES, steps, 2 * nk), lambda i: (0, i, 0)),
        ],
        out_shape=[
            jax.ShapeDtypeStruct((SUBLANES, t_all, nq), BF16),
            jax.ShapeDtypeStruct((SUBLANES, t_all, nk), BF16),
            jax.ShapeDtypeStruct((SUBLANES, t_all, 2 * nk), BF16),
        ],
        compiler_params=_cparams(("arbitrary",)),
        name="attn_qkv",
    )(xs, mod, g, pm, w, gains, same, cos, sin)


def _attn_kernel(q_ref, k_ref, v_ref, o_ref, *, head_dim):
    tq, nq = q_ref.shape
    n_heads = nq // head_dim
    q_per_kv = n_heads // (k_ref.shape[1] // LANES)
    lane = lax.broadcasted_iota(jnp.int32, (tq, LANES), 1)
    low_head = lane < head_dim
    first_slot = (lane & (head_dim // 2)) == 0

    def scores(h):
        j, hi_head = divmod(h, 2)
        kv = h // q_per_kv
        qblk = q_ref[:, j * LANES:(j + 1) * LANES].astype(F32)
        keep = jnp.logical_not(first_slot) if hi_head else first_slot
        qh = jnp.where(keep, qblk, 0.0).astype(BF16)
        k = k_ref[:, kv * LANES:(kv + 1) * LANES]
        return lax.dot_general(qh, k, (((1,), (1,)), ((), ())), preferred_element_type=F32)

    ahead = [scores(h) for h in range(min(SCORE_LOOKAHEAD, n_heads))]
    lo = None
    for h in range(n_heads):
        s = ahead.pop(0)
        if h + SCORE_LOOKAHEAD < n_heads:
            ahead.append(scores(h + SCORE_LOOKAHEAD))
        e = jnp.exp2(s - jnp.max(s, axis=-1, keepdims=True)).astype(BF16)
        vcol = (2 * (h // q_per_kv) + h % 2) * LANES
        res = jnp.dot(e, v_ref[:, vcol:vcol + LANES], preferred_element_type=F32)
        if h % 2 == 0:
            lo = res
        else:
            num = jnp.where(low_head, lo, res)
            den = pltpu.roll(jnp.where(low_head, res, lo), head_dim, axis=1)
            j = h // 2
            o_ref[:, j * LANES:(j + 1) * LANES] = (num / den).astype(BF16)


def _attention(q, k2, v2, q_start, q_len, q_tile, key_start, key_len, head_dim):
    n_batch, _, nq = q.shape
    assert q_start % q_tile == 0 and q_len % q_tile == 0 and key_start % key_len == 0
    q0 = q_start // q_tile
    k0 = key_start // key_len
    return pl.pallas_call(
        functools.partial(_attn_kernel, head_dim=head_dim),
        grid=(n_batch, q_len // q_tile),
        in_specs=[
            pl.BlockSpec((None, q_tile, nq), lambda b, i: (b, i + q0, 0)),
            pl.BlockSpec((None, key_len, k2.shape[2]), lambda b, i: (b, k0, 0)),
            pl.BlockSpec((None, key_len, v2.shape[2]), lambda b, i: (b, k0, 0)),
        ],
        out_specs=pl.BlockSpec((None, q_tile, nq), lambda b, i: (b, i, 0)),
        out_shape=jax.ShapeDtypeStruct((n_batch, q_len, nq), BF16),
        compiler_params=_cparams(("arbitrary", "arbitrary")),
        name="attention",
    )(q, k2, v2)


def _attn_tail(o_lat, o_ctx, xs, mod, layer, g, w_o, attn_layer, w1, w2, final_g=None):
    n_batch, n_lat, d = o_lat.shape
    steps = ROW_TILE // n_batch
    n_lat_tiles = n_lat // steps
    n_tiles = n_lat_tiles + (o_ctx.shape[1] // steps if o_ctx is not None else 0)
    pm = _row_perm(ROW_TILE, to_batch_major=False)
    operands = [o_lat]
    specs = [pl.BlockSpec((n_batch, steps, d), lambda i: (0, jnp.minimum(i, n_lat_tiles - 1), 0))]
    if o_ctx is not None:
        operands.append(o_ctx)
        specs.append(
            pl.BlockSpec((n_batch, steps, d), lambda i: (0, jnp.maximum(i - n_lat_tiles, 0), 0)))
    operands += [xs, mod, g, pm, w_o, w1, w2]
    specs += [
        pl.BlockSpec((ROW_TILE, d), lambda i: (i, 0)),
        pl.BlockSpec((None, None, SUBLANES, mod.shape[-1]),
                     lambda i: (layer, jnp.where(i < n_lat_tiles, 0, 1), 0, 0)),
        _resident(g), _resident(pm), _resident(w_o, attn_layer), _resident(w1, layer),
        _resident(w2, layer),
    ]
    if final_g is not None:
        assert o_ctx is None
        pmf = _row_perm(ROW_TILE, to_batch_major=True)
        operands += [final_g, pmf]
        specs += [_resident(final_g), _resident(pmf)]
        out_spec = pl.BlockSpec((n_batch, steps, d), lambda i: (0, i, 0))
        out_shape = jax.ShapeDtypeStruct((n_batch, n_lat, d), F32)
    else:
        out_spec = pl.BlockSpec((ROW_TILE, d), lambda i: (i, 0))
        out_shape = jax.ShapeDtypeStruct((n_tiles * ROW_TILE, d), F32)
    return pl.pallas_call(
        functools.partial(_attn_tail_kernel, n_lat_tiles=n_lat_tiles, has_ctx=o_ctx is not None,
                          final=final_g is not None),
        grid=(n_tiles,),
        in_specs=specs,
        out_specs=out_spec,
        out_shape=out_shape,
        compiler_params=_cparams(("arbitrary",)),
        name="attn_tail",
    )(*operands)


def _final_kernel(x_ref, g_ref, pm_ref, o_ref):
    x = x_ref[...]
    y = (x * _rms(x)) * g_ref[...]
    o_ref[...] = _permute_exact(pm_ref[...], y).reshape(o_ref.shape)


def _final_norm(xs, g, n_batch, seq):
    d = xs.shape[1]
    steps = ROW_TILE // n_batch
    pm = _row_perm(ROW_TILE, to_batch_major=True)
    return pl.pallas_call(
        _final_kernel,
        grid=(seq // steps,),
        in_specs=[
            pl.BlockSpec((ROW_TILE, d), lambda j: (j, 0)),
            pl.BlockSpec((1, d), lambda j: (0, 0)),
            pl.BlockSpec(pm.shape, lambda j: (0, 0)),
        ],
        out_specs=pl.BlockSpec((n_batch, steps, d), lambda j: (0, j, 0)),
        out_shape=jax.ShapeDtypeStruct((n_batch, seq, d), F32),
        compiler_params=_cparams(("arbitrary",)),
        name="final_norm",
    )(xs, g, pm)


def _block_diag(blocks, per_block):
    *lead, g, r, c = blocks.shape
    nb = g // per_block
    eye = jnp.eye(per_block, dtype=blocks.dtype)
    out = jnp.einsum('...ngrc,gk->...ngrkc', blocks.reshape(*lead, nb, per_block, r, c), eye)
    return out.reshape(*lead, nb, per_block * r, per_block * c)


def _s5_params(a_re, a_im, log_dt, b_re, b_im, c_re, c_im):
    per_block = S5_OUT_BLOCK // S5_GROUP
    ar = a_re.astype(F32)
    ai = a_im.astype(F32)
    dt = jnp.exp(log_dt.astype(F32))[..., None]
    mag = jnp.exp(dt * ar)
    abar_re = mag * jnp.cos(dt * ai)
    abar_im = mag * jnp.sin(dt * ai)
    den = ar * ar + ai * ai
    nr = abar_re - 1.0
    f_re = (nr * ar + abar_im * ai) / den
    f_im = (abar_im * ar - nr * ai) / den
    br = b_re.astype(F32)
    bi = b_im.astype(F32)
    bb_re = f_re[..., None] * br - f_im[..., None] * bi
    bb_im = f_re[..., None] * bi + f_im[..., None] * br
    assert S5_IN_BLOCK == S5_OUT_BLOCK
    bw = jnp.concatenate([_block_diag(bb_re.astype(BF16), per_block),
                          _block_diag(bb_im.astype(BF16), per_block)], axis=-2)
    cw = jnp.concatenate([_block_diag(jnp.swapaxes(c_re, -1, -2).astype(BF16), per_block),
                          _block_diag(-jnp.swapaxes(c_im, -1, -2).astype(BF16), per_block)], axis=-2)
    lead = ar.shape[:2]
    ab = jnp.stack([abar_re.reshape(*lead, -1), abar_im.reshape(*lead, -1)], axis=2)
    ab = jnp.broadcast_to(ab[:, :, :, None, :], (*lead, 2, SUBLANES, ab.shape[-1]))
    return ab, bw, cw


def _head_lane_layout(head_dim):
    quarter = head_dim // 4
    lane = np.arange(LANES)
    second = lane // (LANES // 2)
    pos = lane % (head_dim // 2)
    dim = np.where(pos < quarter, pos, 2 * quarter + (pos - quarter)) + quarter * second
    slot = (lane % (LANES // 2)) // (head_dim // 2)
    return dim, slot


def _rope_tables(n_ctx, seq, head_dim):
    quarter = head_dim // 4
    pos = jnp.arange(seq, dtype=jnp.int32)
    row = (pos // GRID_W).astype(F32)
    col = (pos % GRID_W).astype(F32)
    inv_freq = ROPE_THETA ** (-jnp.arange(quarter, dtype=F32) / quarter)
    ang = jnp.concatenate([row[:, None] * inv_freq[None, :], col[:, None] * inv_freq[None, :]],
                          axis=-1)
    ang = jnp.tile(ang, (1, LANES // ang.shape[1]))
    sign = jnp.where(jnp.arange(LANES) < LANES // 2, -1.0, 1.0)
    cos = jnp.concatenate([jnp.cos(ang), jnp.ones((n_ctx, LANES), F32)], axis=0)
    sin = jnp.concatenate([jnp.sin(ang) * sign, jnp.zeros((n_ctx, LANES), F32)], axis=0)
    return cos, sin


def _qkv_params(w_qkv, q_g, k_g, head_dim):
    nq = N_HEADS * head_dim
    nkv = N_KV_HEADS * head_dim
    per_block = LANES // head_dim
    dim, slot = _head_lane_layout(head_dim)
    q_cols = np.concatenate([(per_block * j + slot) * head_dim + dim
                             for j in range(N_HEADS // per_block)])
    k_cols = np.concatenate([nq + h * head_dim + dim for h in range(N_KV_HEADS)])
    v_cols = np.concatenate([nq + nkv + h * head_dim + np.arange(LANES) % head_dim
                             for h in range(N_KV_HEADS)])
    w = w_qkv.astype(BF16)[..., np.concatenate([q_cols, k_cols, v_cols])]
    q_scale = (head_dim ** -0.5) * LOG2_E
    gains = jnp.stack([q_g[..., dim] * q_scale, k_g[..., dim]], axis=-2)
    return w, gains


def kernel(x, c, ctx, c_ctx, ada_w, ada_b, norm_mix_g, norm_ffn_g, s5_a_re, s5_a_im, s5_log_dt,
           s5_b_re, s5_b_im, s5_c_re, s5_c_im, s5_d, s5_w_glu, attn_w_qkv, attn_q_g, attn_k_g,
           attn_w_o, ffn_w1, ffn_w2, final_g):
    n_batch, seq, d = x.shape
    n_ctx = ctx.shape[1]
    depth = ada_w.shape[0]
    head_dim = d // N_HEADS
    t_all = n_ctx + seq
    assert n_batch == SUBLANES and seq % Q_TILE == 0 and seq % n_ctx == 0
    assert n_ctx % SCAN_STEPS == 0 and seq % SCAN_STEPS == 0 and d % S5_IN_BLOCK == 0
    lat_row_tiles = seq * n_batch // ROW_TILE

    xs = _to_stream(x, ctx)
    c2 = jnp.concatenate([c, jnp.broadcast_to(c_ctx[None, :], (SUBLANES, d))], axis=0)
    mod = _adaln(c2, ada_w, ada_b).reshape(depth, 2, SUBLANES, 6 * d)
    cos, sin = _rope_tables(n_ctx, seq, head_dim)

    w1 = ffn_w1.astype(BF16)
    w2 = ffn_w2.astype(BF16)
    w_glu = s5_w_glu.astype(BF16)
    w_o = attn_w_o.astype(BF16)
    s5_abar, s5_bw, s5_cw = _s5_params(s5_a_re, s5_a_im, s5_log_dt, s5_b_re, s5_b_im, s5_c_re, s5_c_im)
    w_qkv, qk_gains = _qkv_params(attn_w_qkv, attn_q_g, attn_k_g, head_dim)

    for i in range(depth):
        last = i == depth - 1
        j = i // 2
        g_mix = norm_mix_g[i].reshape(1, d)
        g_ffn = norm_ffn_g[i].reshape(1, d)
        if i % 2 == 0:
            dvec = s5_d[j].reshape(1, d)
            yf = _s5_scan(xs, mod, i, g_mix, dvec, s5_abar, s5_bw, s5_cw, j, n_ctx, reverse=False)
            yb = _s5_scan(xs, mod, i, g_mix, dvec, s5_abar, s5_bw, s5_cw, j, n_ctx, reverse=True)
            xs = _s5_tail(yf, yb, xs, mod, i, g_ffn, w_glu, j, w1, w2, seq * n_batch)
        else:
            q, k2, v2 = _qkv(xs, mod, i, g_mix, w_qkv, qk_gains, j, cos, sin, lat_row_tiles, head_dim)
            o_lat = _attention(q, k2, v2, 0, seq, Q_TILE, 0, t_all, head_dim)
            o_ctx = None if last else _attention(q, k2, v2, seq, n_ctx, n_ctx, seq, n_ctx, head_dim)
            xs = _attn_tail(o_lat, o_ctx, xs, mod, i, g_ffn, w_o, j, w1, w2,
                            final_g.reshape(1, d) if last else None)
            if last:
                return xs

    return _final_norm(xs, final_g.reshape(1, d), n_batch, seq)
```
